```python
import math
import jax, jax.numpy as jnp
from jax import lax
import numpy as np

D_MODEL = 2048
BATCH = 2
SEQ = 8192
DEPTH = 4

N_MIXERS = 3
HEAD_DIM = 128
A_Q_HEADS = D_MODEL // HEAD_DIM
A_KV_HEADS = 4
A_BRANCHES = ((128, 1), (512, 4), (2048, 16))
A_SEQ_MULT = max(w for w, _ in A_BRANCHES)
A_QKV_WIDTH = D_MODEL + len(A_BRANCHES) * 2 * A_KV_HEADS * HEAD_DIM
B_HEADS = D_MODEL // (2 * HEAD_DIM)
B_QBLOCK = 128
C_CONV_WIDTH = 3
D_FF = ((8 * D_MODEL // 3 + 255) // 256) * 256
PLE_DIM = 256
N_A = (DEPTH + 2) // 3
N_B = (DEPTH + 1) // 3
N_C = DEPTH // 3
RMS_EPS = 1e-6
SUBLN_EPS = 1e-5

kernel_name = "hybrid_dilated_diffattn_shortconv_macaron"


def rmsnorm(x, g, eps=RMS_EPS):
    xf = x.astype(jnp.float32)
    y = xf * lax.rsqrt(jnp.mean(xf * xf, axis=-1, keepdims=True) + eps)
    return (y * g.astype(jnp.float32)).astype(x.dtype)


def swiglu(h, w_in, w_out):
    gate, up = jnp.split(h @ w_in, 2, axis=-1)
    return (jax.nn.silu(gate) * up) @ w_out


def _to_residues(t, d):
    b, sp = t.shape[:2]
    rest = t.shape[2:]
    t = t.reshape((b, sp // d, d) + rest)
    t = jnp.moveaxis(t, 2, 1)
    return t.reshape((b * d, sp // d) + rest)


def _from_residues(t, b, d):
    n, l = t.shape[:2]
    rest = t.shape[2:]
    t = t.reshape((b, d, l) + rest)
    t = jnp.moveaxis(t, 1, 2)
    return t.reshape((b, l * d) + rest)


def _banded_causal_attention(q, k, v, block):
    n, length, hq, hd = q.shape
    hkv = k.shape[2]
    g = hq // hkv
    nb = length // block
    qb = q.reshape(n, nb, block, hkv, g, hd)

    def with_prev(t):
        tb = t.reshape(n, nb, block, hkv, hd)
        prev = jnp.pad(tb, ((0, 0), (1, 0), (0, 0), (0, 0), (0, 0)))[:, :nb]
        return jnp.concatenate([prev, tb], axis=2)

    kk = with_prev(k)
    vv = with_prev(v)
    s = jnp.einsum('nbqhgd,nbkhd->nbhgqk', qb, kk).astype(jnp.float32) * (hd ** -0.5)
    qi = jnp.arange(block)[:, None] + block
    kj = jnp.arange(2 * block)[None, :]
    dist = qi - kj
    band = (dist >= 0) & (dist <= block)
    first = jnp.arange(nb)[:, None, None] == 0
    mask = band[None] & ~(first & (kj < block)[None])
    s = jnp.where(mask[None, :, None, None], s, -jnp.inf)
    m = jnp.max(s, axis=-1)
    pr = jnp.exp(s - m[..., None])
    l = jnp.sum(pr, axis=-1)
    acc = jnp.einsum('nbhgqk,nbkhd->nbqhgd', pr.astype(v.dtype), vv).astype(jnp.float32)
    m = m.transpose(0, 1, 4, 2, 3).reshape(n, length, hq)
    l = l.transpose(0, 1, 4, 2, 3).reshape(n, length, hq)
    acc = acc.reshape(n, length, hq, hd)
    return m, l, acc


def dilated_sliding_window_attention(h, w_qkv, w_o):
    b, s, _ = h.shape
    sp = -(-s // A_SEQ_MULT) * A_SEQ_MULT
    qkv = jnp.pad(h @ w_qkv, ((0, 0), (0, sp - s), (0, 0)))
    q = qkv[..., :D_MODEL].reshape(b, sp, A_Q_HEADS, HEAD_DIM)
    kv = qkv[..., D_MODEL:].reshape(b, sp, len(A_BRANCHES), 2, A_KV_HEADS, HEAD_DIM)
    ms, ls, accs = [], [], []
    for gi, (window, dil) in enumerate(A_BRANCHES):
        m, l, acc = _banded_causal_attention(
            _to_residues(q, dil), _to_residues(kv[:, :, gi, 0], dil),
            _to_residues(kv[:, :, gi, 1], dil), window // dil)
        ms.append(_from_residues(m, b, dil)[:, :s])
        ls.append(_from_residues(l, b, dil)[:, :s])
        accs.append(_from_residues(acc, b, dil)[:, :s])
    m = jnp.stack(ms)
    l = jnp.stack(ls)
    acc = jnp.stack(accs)
    wgt = jnp.exp(m - jnp.max(m, axis=0, keepdims=True))
    out = jnp.sum(wgt[..., None] * acc, axis=0) / jnp.sum(wgt * l, axis=0)[..., None]
    return out.reshape(b, s, D_MODEL).astype(h.dtype) @ w_o


def differential_attention(h, w_qkv, w_o, lam_params, subln_g, lambda_init):
    b, s, _ = h.shape
    nb = s // B_QBLOCK
    qkv = h @ w_qkv
    q = qkv[..., :D_MODEL].reshape(b, s, B_HEADS, 2, HEAD_DIM)
    k = qkv[..., D_MODEL:2 * D_MODEL].reshape(b, s, B_HEADS, 2, HEAD_DIM)
    v = qkv[..., 2 * D_MODEL:].reshape(b, s, B_HEADS, 2 * HEAD_DIM)
    lp = lam_params.astype(jnp.float32)
    lam = jnp.exp(jnp.sum(lp[0] * lp[1])) - jnp.exp(jnp.sum(lp[2] * lp[3])) + lambda_init
    qb = q.reshape(b, nb, B_QBLOCK, B_HEADS, 2, HEAD_DIM).transpose(1, 0, 3, 4, 2, 5)
    kt = k.transpose(0, 2, 3, 1, 4)
    vt = v.transpose(0, 2, 1, 3)
    kpos = jnp.arange(s)
    scale = HEAD_DIM ** -0.5

    def attend_block(args):
        qblk, bi = args
        sc = jnp.einsum('bhcqd,bhckd->bhcqk', qblk, kt).astype(jnp.float32) * scale
        qpos = bi * B_QBLOCK + jnp.arange(B_QBLOCK)
        sc = jnp.where(kpos[None, :] <= qpos[:, None], sc, -jnp.inf)
        a = jax.nn.softmax(sc, axis=-1)
        diff = a[:, :, 0] - lam * a[:, :, 1]
        return jnp.einsum('bhqk,bhkd->bhqd', diff.astype(vt.dtype), vt)

    o = lax.map(attend_block, (qb, jnp.arange(nb)))
    o = o.transpose(1, 0, 3, 2, 4).reshape(b, s, B_HEADS, 2 * HEAD_DIM)
    o = rmsnorm(o, subln_g, SUBLN_EPS) * (1.0 - lambda_init)
    return o.reshape(b, s, D_MODEL) @ w_o


def short_gated_conv(h, w_in, conv_w, w_out):
    bg, cg, u = jnp.split(h @ w_in, 3, axis=-1)
    z = cg * u
    conv = lax.conv_general_dilated(
        z, conv_w[:, None, :].astype(z.dtype), window_strides=(1,),
        padding=((C_CONV_WIDTH - 1, 0),), dimension_numbers=('NWC', 'WIO', 'NWC'),
        feature_group_count=D_MODEL)
    return (bg * conv) @ w_out


def setup_inputs(seed: int = 0) -> dict:
    key = jax.random.key(seed)
    ks = jax.random.split(key, 32)

    def nrm(k, shape, scale):
        return jax.random.normal(k, shape, jnp.float32) * scale

    def gain(k, shape):
        return 1.0 + 0.02 * jax.random.normal(k, shape, jnp.float32)

    D, F = D_MODEL, D_FF
    return {
        "x": nrm(ks[0], (BATCH, SEQ, D), 1.0),
        "p": nrm(ks[1], (DEPTH, BATCH, SEQ, PLE_DIM), 1.0),
        "norm_ffn1": gain(ks[2], (DEPTH, D)),
        "w_ffn1_in": nrm(ks[3], (DEPTH, D, 2 * F), D ** -0.5),
        "w_ffn1_out": nrm(ks[4], (DEPTH, F, D), F ** -0.5),
        "norm_mix": gain(ks[5], (DEPTH, D)),
        "a_w_qkv": nrm(ks[6], (N_A, D, A_QKV_WIDTH), D ** -0.5),
        "a_w_o": nrm(ks[7], (N_A, D, D), D ** -0.5),
        "b_w_qkv": nrm(ks[8], (N_B, D, 3 * D), D ** -0.5),
        "b_w_o": nrm(ks[9], (N_B, D, D), D ** -0.5),
        "b_lambda": nrm(ks[10], (N_B, 4, HEAD_DIM), 0.1),
        "b_subln": gain(ks[11], (N_B, 2 * HEAD_DIM)),
        "c_w_in": nrm(ks[12], (N_C, D, 3 * D), D ** -0.5),
        "c_conv_w": nrm(ks[13], (N_C, C_CONV_WIDTH, D), C_CONV_WIDTH ** -0.5),
        "c_w_out": nrm(ks[14], (N_C, D, D), D ** -0.5),
        "norm_ffn2": gain(ks[15], (DEPTH, D)),
        "w_ffn2_in": nrm(ks[16], (DEPTH, D, 2 * F), D ** -0.5),
        "w_ffn2_out": nrm(ks[17], (DEPTH, F, D), F ** -0.5),
        "norm_ple": gain(ks[18], (DEPTH, D)),
        "w_ple_gate": nrm(ks[19], (DEPTH, D, D), D ** -0.5),
        "b_ple_gate": nrm(ks[20], (DEPTH, D), 0.02),
        "w_ple_proj": nrm(ks[21], (DEPTH, PLE_DIM, D), PLE_DIM ** -0.5),
        "norm_f": gain(ks[22], (D,)),
    }


def reference(x, p, norm_ffn1, w_ffn1_in, w_ffn1_out, norm_mix, a_w_qkv, a_w_o,
              b_w_qkv, b_w_o, b_lambda, b_subln, c_w_in, c_conv_w, c_w_out,
              norm_ffn2, w_ffn2_in, w_ffn2_out, norm_ple, w_ple_gate, b_ple_gate,
              w_ple_proj, norm_f):
    h = x
    for i in range(DEPTH):
        h = h + 0.5 * swiglu(rmsnorm(h, norm_ffn1[i]), w_ffn1_in[i], w_ffn1_out[i])
        hn = rmsnorm(h, norm_mix[i])
        kind, j = i % N_MIXERS, i // N_MIXERS
        if kind == 0:
            mix = dilated_sliding_window_attention(hn, a_w_qkv[j], a_w_o[j])
        elif kind == 1:
            lambda_init = 0.8 - 0.6 * math.exp(-0.3 * i)
            mix = differential_attention(hn, b_w_qkv[j], b_w_o[j], b_lambda[j],
                                         b_subln[j], lambda_init)
        else:
            mix = short_gated_conv(hn, c_w_in[j], c_conv_w[j], c_w_out[j])
        h = h + mix
        h = h + 0.5 * swiglu(rmsnorm(h, norm_ffn2[i]), w_ffn2_in[i], w_ffn2_out[i])
        gate = jax.nn.sigmoid(rmsnorm(h, norm_ple[i]) @ w_ple_gate[i] + b_ple_gate[i])
        h = h + gate * (p[i] @ w_ple_proj[i])
    return rmsnorm(h, norm_f)
```

```python
import functools
import math

import jax
import jax.numpy as jnp
from jax import lax
from jax.experimental import pallas as pl
from jax.experimental.pallas import tpu as pltpu

BF16 = jnp.bfloat16
F32 = jnp.float32

HEAD_DIM = 128
A_KV_HEADS = 4
A_BRANCHES = ((128, 1), (512, 4), (2048, 16))
A_BLOCK = 128
RMS_EPS = 1e-6
SUBLN_EPS = 1e-5
N_MIXERS = 3

V7X_VMEM_BYTES = 64 * 1024 * 1024
MIB = 1024 * 1024


def _params(semantics, vmem_mib):
    return pltpu.CompilerParams(dimension_semantics=semantics,
                                vmem_limit_bytes=vmem_mib * MIB)


def _rms_normalize(x, gain, eps):
    ms = jnp.mean(x * x, axis=-1, keepdims=True)
    return x * lax.rsqrt(ms + eps) * gain


def _sigmoid(x):
    return 1.0 / (1.0 + jnp.exp(-x))


def _ffn_kernel(h_ref, g_ref, wg_ref, wu_ref, wo_ref, o_ref, hn_ref):
    @pl.when(pl.program_id(1) == 0)
    def _():
        h = h_ref[...]
        hn_ref[...] = _rms_normalize(h, g_ref[...], RMS_EPS).astype(BF16)
        o_ref[...] = h

    hn = hn_ref[...]
    gate = jnp.dot(hn, wg_ref[...], preferred_element_type=F32)
    up = jnp.dot(hn, wu_ref[...], preferred_element_type=F32)
    act = (0.5 * gate * _sigmoid(gate)) * up
    o_ref[...] += jnp.dot(act.astype(BF16), wo_ref[...], preferred_element_type=F32)


def _ffn(h, gain, w_in, w_out, layer, tm, tf):
    m, d = h.shape
    f = w_out.shape[1]
    nf = f // tf
    return pl.pallas_call(
        _ffn_kernel,
        grid=(m // tm, nf),
        in_specs=[
            pl.BlockSpec((tm, d), lambda i, j: (i, 0)),
            pl.BlockSpec((None, 1, d), lambda i, j: (layer, 0, 0)),
            pl.BlockSpec((None, d, tf), lambda i, j: (layer, 0, j)),
            pl.BlockSpec((None, d, tf), lambda i, j: (layer, 0, nf + j)),
            pl.BlockSpec((None, tf, d), lambda i, j: (layer, j, 0)),
        ],
        out_specs=pl.BlockSpec((tm, d), lambda i, j: (i, 0)),
        out_shape=jax.ShapeDtypeStruct((m, d), F32),
        scratch_shapes=[pltpu.VMEM((tm, d), BF16)],
        compiler_params=_params(("parallel", "arbitrary"), 56),
        name="ffn",
    )(h, gain, w_in, w_in, w_out)


def _norm_matmul_kernel(h_ref, g_ref, w_ref, o_ref, hn_ref):
    @pl.when(pl.program_id(1) == 0)
    def _():
        hn_ref[...] = _rms_normalize(h_ref[...], g_ref[...], RMS_EPS).astype(BF16)

    o_ref[...] = jnp.dot(hn_ref[...], w_ref[...],
                         preferred_element_type=F32).astype(o_ref.dtype)


def _norm_matmul(h, gain, w, layer, wlayer, out_dtype, tm, tn):
    m, d = h.shape
    n = w.shape[2]
    return pl.pallas_call(
        _norm_matmul_kernel,
        grid=(m // tm, n // tn),
        in_specs=[
            pl.BlockSpec((tm, d), lambda i, j: (i, 0)),
            pl.BlockSpec((None, 1, d), lambda i, j: (layer, 0, 0)),
            pl.BlockSpec((None, d, tn), lambda i, j: (wlayer, 0, j)),
        ],
        out_specs=pl.BlockSpec((tm, tn), lambda i, j: (i, j)),
        out_shape=jax.ShapeDtypeStruct((m, n), out_dtype),
        scratch_shapes=[pltpu.VMEM((tm, d), BF16)],
        compiler_params=_params(("parallel", "arbitrary"), 48),
        name="norm_matmul",
    )(h, gain, w)


def _proj_residual_kernel(x_ref, w_ref, h_ref, o_ref):
    o_ref[...] = h_ref[...] + jnp.dot(x_ref[...], w_ref[...],
                                      preferred_element_type=F32)


def _proj_residual(x, w, wlayer, h, tm, tn):
    m, k = x.shape
    n = w.shape[2]
    return pl.pallas_call(
        _proj_residual_kernel,
        grid=(m // tm, n // tn),
        in_specs=[
            pl.BlockSpec((tm, k), lambda i, j: (i, 0)),
            pl.BlockSpec((None, k, tn), lambda i, j: (wlayer, 0, j)),
            pl.BlockSpec((tm, tn), lambda i, j: (i, j)),
        ],
        out_specs=pl.BlockSpec((tm, tn), lambda i, j: (i, j)),
        out_shape=jax.ShapeDtypeStruct((m, n), F32),
        compiler_params=_params(("parallel", "arbitrary"), 48),
        name="proj_residual",
    )(x, w, h)


def _ple_kernel(h_ref, g_ref, wg_ref, b_ref, p_ref, wp_ref, gf_ref, o_ref, *, final_norm):
    h = h_ref[...]
    hn = _rms_normalize(h, g_ref[...], RMS_EPS).astype(BF16)
    gate = _sigmoid(jnp.dot(hn, wg_ref[...], preferred_element_type=F32) + b_ref[...])
    proj = jnp.dot(p_ref[...].astype(BF16), wp_ref[...], preferred_element_type=F32)
    out = h + gate * proj
    if final_norm:
        out = _rms_normalize(out, gf_ref[...], RMS_EPS)
    o_ref[...] = out


def _ple(h, gain, w_gate, b_gate, p, w_proj, gain_f, layer, final_norm, tm):
    m, d = h.shape
    e = p.shape[2]
    nblk = m // tm
    return pl.pallas_call(
        functools.partial(_ple_kernel, final_norm=final_norm),
        grid=(nblk,),
        in_specs=[
            pl.BlockSpec((tm, d), lambda i: (i, 0)),
            pl.BlockSpec((None, 1, d), lambda i: (layer, 0, 0)),
            pl.BlockSpec((None, d, d), lambda i: (layer, 0, 0)),
            pl.BlockSpec((None, 1, d), lambda i: (layer, 0, 0)),
            pl.BlockSpec((None, tm, e), lambda i: (layer, i, 0)),
            pl.BlockSpec((None, e, d), lambda i: (layer, 0, 0)),
            pl.BlockSpec((1, d), lambda i: (0, 0)),
        ],
        out_specs=pl.BlockSpec((tm, d), lambda i: (i, 0)),
        out_shape=jax.ShapeDtypeStruct((m, d), F32),
        compiler_params=_params(("parallel",), 56),
        name="ple",
    )(h, gain, w_gate, b_gate, p, w_proj, gain_f)


def _dilated_branch_kernel(q0_ref, q1_ref, q2_ref, q3_ref, kp_ref, kc_ref, vp_ref, vc_ref,
                           acc_ref, st_ref):
    blk = A_BLOCK
    first = pl.program_id(2) == 0
    heads_per_group = len((q0_ref, q1_ref, q2_ref, q3_ref))
    rows = heads_per_group * blk
    qi = lax.broadcasted_iota(jnp.int32, (rows, 2 * blk), 0) % blk
    kj = lax.broadcasted_iota(jnp.int32, (rows, 2 * blk), 1)
    in_prev = (kj < blk) & (kj >= qi) & jnp.logical_not(first)
    in_cur = (kj >= blk) & (kj - blk <= qi)
    mask = in_prev | in_cur
    scale = HEAD_DIM ** -0.5
    for g, q_ref in enumerate((q0_ref, q1_ref, q2_ref, q3_ref)):
        lanes = slice(g * HEAD_DIM, (g + 1) * HEAD_DIM)
        q = jnp.concatenate(
            [q_ref[0, :, hl * HEAD_DIM:(hl + 1) * HEAD_DIM] for hl in range(heads_per_group)],
            axis=0)
        k = jnp.concatenate([kp_ref[0, :, lanes], kc_ref[0, :, lanes]], axis=0)
        v = jnp.concatenate([vp_ref[0, :, lanes], vc_ref[0, :, lanes]], axis=0)
        s = lax.dot_general(q, k, (((1,), (1,)), ((), ())),
                            preferred_element_type=F32) * scale
        s = jnp.where(mask, s, -jnp.inf)
        m = jnp.max(s, axis=-1, keepdims=True)
        p = jnp.exp(s - m)
        l = jnp.sum(p, axis=-1, keepdims=True)
        acc = jnp.dot(p.astype(BF16), v, preferred_element_type=F32)
        for hl in range(heads_per_group):
            head = g * heads_per_group + hl
            r = slice(hl * blk, (hl + 1) * blk)
            acc_ref[0, :, head * HEAD_DIM:(head + 1) * HEAD_DIM] = acc[r]
            st_ref[0, :, head:head + 1] = m[r]
            st_ref[0, :, 16 + head:17 + head] = l[r]


def _dilated_branch(qkv, branch, dil):
    b, s, c = qkv.shape
    d = 4 * A_KV_HEADS * HEAD_DIM
    group_w = 4 * HEAD_DIM
    cw = c // group_w
    kblk = d // group_w + 2 * branch
    n_j = s // dil // A_BLOCK
    view = qkv.reshape(b, s // dil, dil * c)

    def q_spec(g):
        return pl.BlockSpec((1, A_BLOCK, group_w), lambda bi, r, j: (bi, j, r * cw + g))

    def kv_spec(off, prev):
        if prev:
            return pl.BlockSpec((1, A_BLOCK, group_w),
                                lambda bi, r, j: (bi, jnp.maximum(j - 1, 0), r * cw + kblk + off))
        return pl.BlockSpec((1, A_BLOCK, group_w), lambda bi, r, j: (bi, j, r * cw + kblk + off))

    acc, st = pl.pallas_call(
        _dilated_branch_kernel,
        grid=(b, dil, n_j),
        in_specs=[q_spec(0), q_spec(1), q_spec(2), q_spec(3),
                  kv_spec(0, True), kv_spec(0, False), kv_spec(1, True), kv_spec(1, False)],
        out_specs=[pl.BlockSpec((1, A_BLOCK, d), lambda bi, r, j: (bi, j, r)),
                   pl.BlockSpec((1, A_BLOCK, 128), lambda bi, r, j: (bi, j, r))],
        out_shape=[jax.ShapeDtypeStruct((b, s // dil, dil * d), F32),
                   jax.ShapeDtypeStruct((b, s // dil, dil * 128), F32)],
        compiler_params=_params(("parallel", "parallel", "arbitrary"), 32),
        name=f"dilated_branch_d{dil}",
    )(view, view, view, view, view, view, view, view)
    return acc.reshape(b * s, d), st.reshape(b * s, 128)


def _combine_kernel(a0_ref, a1_ref, a2_ref, s0_ref, s1_ref, s2_ref, o_ref):
    accs = (a0_ref, a1_ref, a2_ref)
    stats = [s_ref[...] for s_ref in (s0_ref, s1_ref, s2_ref)]
    n_heads = o_ref.shape[1] // HEAD_DIM
    for head in range(n_heads):
        ms = [st[:, head:head + 1] for st in stats]
        ls = [st[:, 16 + head:17 + head] for st in stats]
        mx = jnp.maximum(jnp.maximum(ms[0], ms[1]), ms[2])
        wg = [jnp.exp(mi - mx) for mi in ms]
        den = wg[0] * ls[0] + wg[1] * ls[1] + wg[2] * ls[2]
        lanes = slice(head * HEAD_DIM, (head + 1) * HEAD_DIM)
        num = (wg[0] * accs[0][:, lanes] + wg[1] * accs[1][:, lanes]
               + wg[2] * accs[2][:, lanes])
        o_ref[:, lanes] = (num / den).astype(o_ref.dtype)


def _combine(accs, stats, tm):
    m, d = accs[0].shape
    acc_spec = pl.BlockSpec((tm, d), lambda i: (i, 0))
    st_spec = pl.BlockSpec((tm, 128), lambda i: (i, 0))
    return pl.pallas_call(
        _combine_kernel,
        grid=(m // tm,),
        in_specs=[acc_spec] * 3 + [st_spec] * 3,
        out_specs=pl.BlockSpec((tm, d), lambda i: (i, 0)),
        out_shape=jax.ShapeDtypeStruct((m, d), BF16),
        compiler_params=_params(("parallel",), 32),
        name="dilated_combine",
    )(*accs, *stats)


def _dilated_attention(qkv, b, s):
    qkv = qkv.reshape(b, s, qkv.shape[-1])
    accs, stats = [], []
    for branch, (_, dil) in enumerate(A_BRANCHES):
        acc, st = _dilated_branch(qkv, branch, dil)
        accs.append(acc)
        stats.append(st)
    return _combine(accs, stats, 256)


def _diff_attention_kernel(q_ref, k_ref, v_ref, lam_ref, g_ref, o_ref,
                           m_ref, l_ref, acc_ref, *, tq, lambda_init):
    qi = pl.program_id(2)
    scale = HEAD_DIM ** -0.5
    m_ref[...] = jnp.full(m_ref.shape, -jnp.inf, F32)
    l_ref[...] = jnp.zeros(l_ref.shape, F32)
    acc_ref[...] = jnp.zeros(acc_ref.shape, F32)
    q = q_ref[0]

    def attend(kc, masked):
        start = pl.multiple_of(kc * tq, tq)
        k = k_ref[0, pl.ds(start, tq), :]
        v = v_ref[0, pl.ds(start, tq), :]
        for c in range(2):
            lanes = slice(c * HEAD_DIM, (c + 1) * HEAD_DIM)
            s = lax.dot_general(q[:, lanes], k[:, lanes], (((1,), (1,)), ((), ())),
                                preferred_element_type=F32) * scale
            if masked:
                row = lax.broadcasted_iota(jnp.int32, s.shape, 0)
                col = lax.broadcasted_iota(jnp.int32, s.shape, 1)
                s = jnp.where(col <= row, s, -jnp.inf)
            m_old = m_ref[c]
            m_new = jnp.maximum(m_old, jnp.max(s, axis=-1, keepdims=True))
            alpha = jnp.exp(m_old - m_new)
            p = jnp.exp(s - m_new)
            l_ref[c] = alpha * l_ref[c] + jnp.sum(p, axis=-1, keepdims=True)
            acc_ref[c] = alpha * acc_ref[c] + jnp.dot(p.astype(BF16), v,
                                                      preferred_element_type=F32)
            m_ref[c] = m_new

    def body(kc, carry):
        attend(kc, False)
        return carry

    lax.fori_loop(0, qi, body, 0)
    attend(qi, True)

    lp = lam_ref[...]
    lam = (jnp.exp(jnp.sum(lp[0:1] * lp[1:2], axis=-1, keepdims=True))
           - jnp.exp(jnp.sum(lp[2:3] * lp[3:4], axis=-1, keepdims=True)) + lambda_init)
    o = acc_ref[0] / l_ref[0] - lam * (acc_ref[1] / l_ref[1])
    o = _rms_normalize(o, g_ref[...], SUBLN_EPS) * (1.0 - lambda_init)
    o_ref[0] = o.astype(o_ref.dtype)


def _diff_attention(qkv, lam_params, subln_g, wlayer, lambda_init, b, s, tq):
    d = qkv.shape[-1] // 3
    hw = 2 * HEAD_DIM
    n_heads = d // hw
    qkv = qkv.reshape(b, s, 3 * d)
    out = pl.pallas_call(
        functools.partial(_diff_attention_kernel, tq=tq, lambda_init=lambda_init),
        grid=(b, n_heads, s // tq),
        in_specs=[
            pl.BlockSpec((1, tq, hw), lambda bi, h, i: (bi, i, h)),
            pl.BlockSpec((1, s, hw), lambda bi, h, i: (bi, 0, n_heads + h)),
            pl.BlockSpec((1, s, hw), lambda bi, h, i: (bi, 0, 2 * n_heads + h)),
            pl.BlockSpec((None, 4, HEAD_DIM), lambda bi, h, i: (wlayer, 0, 0)),
            pl.BlockSpec((None, 1, hw), lambda bi, h, i: (wlayer, 0, 0)),
        ],
        out_specs=pl.BlockSpec((1, tq, hw), lambda bi, h, i: (bi, i, h)),
        out_shape=jax.ShapeDtypeStruct((b, s, d), BF16),
        scratch_shapes=[pltpu.VMEM((2, tq, 1), F32), pltpu.VMEM((2, tq, 1), F32),
                        pltpu.VMEM((2, tq, hw), F32)],
        compiler_params=_params(("parallel", "parallel", "arbitrary"), 40),
        name="diff_attention",
    )(qkv, qkv, qkv, lam_params, subln_g)
    return out.reshape(b * s, d)


def _conv_proj_kernel(bg_ref, cg_ref, u_ref, cgh_ref, uh_ref, cw_ref, w_ref, h_ref, o_ref,
                      z_ref, *, tiles_per_seq):
    tm = bg_ref.shape[0]
    halo = cgh_ref.shape[0]
    seq_start = pl.program_id(0) % tiles_per_seq == 0
    zh = cgh_ref[...] * uh_ref[...]
    z_ref[0:halo, :] = jnp.where(seq_start, 0.0, zh)
    z_ref[halo:, :] = cg_ref[...] * u_ref[...]
    cw = cw_ref[...]
    conv = (cw[0:1] * z_ref[pl.ds(halo - 2, tm), :]
            + cw[1:2] * z_ref[pl.ds(halo - 1, tm), :]
            + cw[2:3] * z_ref[pl.ds(halo, tm), :])
    y = (bg_ref[...] * conv).astype(BF16)
    o_ref[...] = h_ref[...] + jnp.dot(y, w_ref[...], preferred_element_type=F32)


def _conv_proj(proj, conv_w, w_out, wlayer, h, s, tm):
    m, d = h.shape
    halo = 8
    ratio = tm // halo
    return pl.pallas_call(
        functools.partial(_conv_proj_kernel, tiles_per_seq=s // tm),
        grid=(m // tm,),
        in_specs=[
            pl.BlockSpec((tm, d), lambda i: (i, 0)),
            pl.BlockSpec((tm, d), lambda i: (i, 1)),
            pl.BlockSpec((tm, d), lambda i: (i, 2)),
            pl.BlockSpec((halo, d), lambda i: (jnp.maximum(i * ratio - 1, 0), 1)),
            pl.BlockSpec((halo, d), lambda i: (jnp.maximum(i * ratio - 1, 0), 2)),
            pl.BlockSpec((None, 3, d), lambda i: (wlayer, 0, 0)),
            pl.BlockSpec((None, d, d), lambda i: (wlayer, 0, 0)),
            pl.BlockSpec((tm, d), lambda i: (i, 0)),
        ],
        out_specs=pl.BlockSpec((tm, d), lambda i: (i, 0)),
        out_shape=jax.ShapeDtypeStruct((m, d), F32),
        scratch_shapes=[pltpu.VMEM((tm + halo, d), F32)],
        compiler_params=_params(("parallel",), 56),
        name="conv_proj",
    )(proj, proj, proj, proj, proj, conv_w, w_out, h)


def kernel(x, p, norm_ffn1, w_ffn1_in, w_ffn1_out, norm_mix, a_w_qkv, a_w_o, b_w_qkv, b_w_o,
           b_lambda, b_subln, c_w_in, c_conv_w, c_w_out, norm_ffn2, w_ffn2_in, w_ffn2_out,
           norm_ple, w_ple_gate, b_ple_gate, w_ple_proj, norm_f):
    b, s, d = x.shape
    depth = p.shape[0]
    m = b * s
    h = x.reshape(m, d)
    p = p.reshape(depth, m, p.shape[-1])

    def row(t):
        return t.reshape(t.shape[0], 1, t.shape[-1])

    norm_ffn1, norm_mix, norm_ffn2, norm_ple, b_ple_gate, b_subln = (
        row(t) for t in (norm_ffn1, norm_mix, norm_ffn2, norm_ple, b_ple_gate, b_subln))
    norm_f = norm_f.reshape(1, d)
    (w_ffn1_in, w_ffn1_out, a_w_qkv, a_w_o, b_w_qkv, b_w_o, c_w_in, c_w_out,
     w_ffn2_in, w_ffn2_out, w_ple_gate, w_ple_proj) = (
        t.astype(BF16) for t in (w_ffn1_in, w_ffn1_out, a_w_qkv, a_w_o, b_w_qkv, b_w_o,
                                 c_w_in, c_w_out, w_ffn2_in, w_ffn2_out, w_ple_gate,
                                 w_ple_proj))

    for i in range(depth):
        h = _ffn(h, norm_ffn1, w_ffn1_in, w_ffn1_out, i, 512, 512)
        kind, j = i % N_MIXERS, i // N_MIXERS
        if kind == 0:
            qkv = _norm_matmul(h, norm_mix, a_w_qkv, i, j, BF16, 1024, 1024)
            mix = _dilated_attention(qkv, b, s)
            h = _proj_residual(mix, a_w_o, j, h, 1024, 1024)
        elif kind == 1:
            lambda_init = 0.8 - 0.6 * math.exp(-0.3 * i)
            qkv = _norm_matmul(h, norm_mix, b_w_qkv, i, j, BF16, 1024, 1024)
            mix = _diff_attention(qkv, b_lambda, b_subln, j, lambda_init, b, s, 512)
            h = _proj_residual(mix, b_w_o, j, h, 1024, 1024)
        else:
            proj = _norm_matmul(h, norm_mix, c_w_in, i, j, F32, 1024, 1024)
            h = _conv_proj(proj, c_conv_w, c_w_out, j, h, s, 256)
        h = _ffn(h, norm_ffn2, w_ffn2_in, w_ffn2_out, i, 512, 512)
        h = _ple(h, norm_ple, w_ple_gate, b_ple_gate, p, w_ple_proj, norm_f, i,
                 i == depth - 1, 512)
    return h.reshape(b, s, d)
```

```python
import functools
import math

import jax
import jax.numpy as jnp
from jax import lax
from jax.experimental import pallas as pl
from jax.experimental.pallas import tpu as pltpu

BF16 = jnp.bfloat16
F32 = jnp.float32

HEAD_DIM = 128
A_Q_PER_KV = 4
A_BRANCHES = ((2048, 16), (512, 4), (128, 1))
A_BRANCH_COLUMN = {1: 0, 4: 1, 16: 2}
A_BLOCK = 128
A_TILE = 2048
RMS_EPS = 1e-6
SUBLN_EPS = 1e-5
N_MIXERS = 3

V7X_VMEM_BYTES = 64 * 1024 * 1024
MIB = 1024 * 1024


def _params(semantics, vmem_mib):
    return pltpu.CompilerParams(dimension_semantics=semantics,
                                vmem_limit_bytes=vmem_mib * MIB)


def _rms_normalize(x, gain, eps):
    ms = jnp.mean(x * x, axis=-1, keepdims=True)
    return x * lax.rsqrt(ms + eps) * gain


def _sigmoid(x):
    return 1.0 / (1.0 + jnp.exp(-x))


def _ffn_kernel(h_ref, g_ref, wg_ref, wu_ref, wo_ref, o_ref, hn_ref):
    @pl.when(pl.program_id(1) == 0)
    def _():
        h = h_ref[...]
        hn_ref[...] = _rms_normalize(h, g_ref[...], RMS_EPS).astype(BF16)
        o_ref[...] = h

    hn = hn_ref[...]
    gate = jnp.dot(hn, wg_ref[...], preferred_element_type=F32)
    up = jnp.dot(hn, wu_ref[...], preferred_element_type=F32)
    act = (0.5 * gate * _sigmoid(gate)) * up
    o_ref[...] += jnp.dot(act.astype(BF16), wo_ref[...], preferred_element_type=F32)


def _ffn(h, gain, w_in, w_out, layer, tm, tf):
    m, d = h.shape
    f = w_out.shape[1]
    nf = f // tf
    return pl.pallas_call(
        _ffn_kernel,
        grid=(m // tm, nf),
        in_specs=[
            pl.BlockSpec((tm, d), lambda i, j: (i, 0)),
            pl.BlockSpec((None, 1, d), lambda i, j: (layer, 0, 0)),
            pl.BlockSpec((None, d, tf), lambda i, j: (layer, 0, j)),
            pl.BlockSpec((None, d, tf), lambda i, j: (layer, 0, nf + j)),
            pl.BlockSpec((None, tf, d), lambda i, j: (layer, j, 0)),
        ],
        out_specs=pl.BlockSpec((tm, d), lambda i, j: (i, 0)),
        out_shape=jax.ShapeDtypeStruct((m, d), F32),
        scratch_shapes=[pltpu.VMEM((tm, d), BF16)],
        compiler_params=_params(("parallel", "arbitrary"), 56),
        name="ffn",
    )(h, gain, w_in, w_in, w_out)


def _norm_matmul_kernel(h_ref, g_ref, w_ref, o_ref, hn_ref):
    @pl.when(pl.program_id(1) == 0)
    def _():
        hn_ref[...] = _rms_normalize(h_ref[...], g_ref[...], RMS_EPS).astype(BF16)

    o_ref[...] = jnp.dot(hn_ref[...], w_ref[...],
                         preferred_element_type=F32).astype(o_ref.dtype)


def _norm_matmul(h, gain, w, layer, wlayer, out_dtype, tm, tn):
    m, d = h.shape
    n = w.shape[2]
    return pl.pallas_call(
        _norm_matmul_kernel,
        grid=(m // tm, n // tn),
        in_specs=[
            pl.BlockSpec((tm, d), lambda i, j: (i, 0)),
            pl.BlockSpec((None, 1, d), lambda i, j: (layer, 0, 0)),
            pl.BlockSpec((None, d, tn), lambda i, j: (wlayer, 0, j)),
        ],
        out_specs=pl.BlockSpec((tm, tn), lambda i, j: (i, j)),
        out_shape=jax.ShapeDtypeStruct((m, n), out_dtype),
        scratch_shapes=[pltpu.VMEM((tm, d), BF16)],
        compiler_params=_params(("parallel", "arbitrary"), 48),
        name="norm_matmul",
    )(h, gain, w)


def _proj_residual_kernel(x_ref, w_ref, h_ref, o_ref):
    o_ref[...] = h_ref[...] + jnp.dot(x_ref[...], w_ref[...],
                                      preferred_element_type=F32)


def _proj_residual(x, w, wlayer, h, tm, tn):
    m, k = x.shape
    n = w.shape[2]
    return pl.pallas_call(
        _proj_residual_kernel,
        grid=(m // tm, n // tn),
        in_specs=[
            pl.BlockSpec((tm, k), lambda i, j: (i, 0)),
            pl.BlockSpec((None, k, tn), lambda i, j: (wlayer, 0, j)),
            pl.BlockSpec((tm, tn), lambda i, j: (i, j)),
        ],
        out_specs=pl.BlockSpec((tm, tn), lambda i, j: (i, j)),
        out_shape=jax.ShapeDtypeStruct((m, n), F32),
        compiler_params=_params(("parallel", "arbitrary"), 48),
        name="proj_residual",
    )(x, w, h)


def _ple_kernel(h_ref, g_ref, wg_ref, b_ref, p_ref, wp_ref, gf_ref, o_ref, *, final_norm):
    h = h_ref[...]
    hn = _rms_normalize(h, g_ref[...], RMS_EPS).astype(BF16)
    gate = _sigmoid(jnp.dot(hn, wg_ref[...], preferred_element_type=F32) + b_ref[...])
    proj = jnp.dot(p_ref[...].astype(BF16), wp_ref[...], preferred_element_type=F32)
    out = h + gate * proj
    if final_norm:
        out = _rms_normalize(out, gf_ref[...], RMS_EPS)
    o_ref[...] = out


def _ple(h, gain, w_gate, b_gate, p, w_proj, gain_f, layer, final_norm, tm):
    m, d = h.shape
    e = p.shape[2]
    nblk = m // tm
    return pl.pallas_call(
        functools.partial(_ple_kernel, final_norm=final_norm),
        grid=(nblk,),
        in_specs=[
            pl.BlockSpec((tm, d), lambda i: (i, 0)),
            pl.BlockSpec((None, 1, d), lambda i: (layer, 0, 0)),
            pl.BlockSpec((None, d, d), lambda i: (layer, 0, 0)),
            pl.BlockSpec((None, 1, d), lambda i: (layer, 0, 0)),
            pl.BlockSpec((None, tm, e), lambda i: (layer, i, 0)),
            pl.BlockSpec((None, e, d), lambda i: (layer, 0, 0)),
            pl.BlockSpec((1, d), lambda i: (0, 0)),
        ],
        out_specs=pl.BlockSpec((tm, d), lambda i: (i, 0)),
        out_shape=jax.ShapeDtypeStruct((m, d), F32),
        compiler_params=_params(("parallel",), 56),
        name="ple",
    )(h, gain, w_gate, b_gate, p, w_proj, gain_f)


def _rows(start, dil):
    if dil == 1:
        return pl.ds(pl.multiple_of(start, A_BLOCK), A_BLOCK)
    return pl.ds(start, A_BLOCK, stride=dil)


def _dilated_attention_kernel(*refs):
    n_q = A_Q_PER_KV
    q_refs = refs[:n_q]
    kv_refs = refs[n_q:n_q + 12]
    o_ref = refs[n_q + 12]
    m_ref, l_ref, acc_ref = refs[n_q + 13:]
    blk = A_BLOCK
    first_tile = pl.program_id(1) == 0
    rows = n_q * blk
    qi = lax.broadcasted_iota(jnp.int32, (rows, 2 * blk), 0) % blk
    kj = lax.broadcasted_iota(jnp.int32, (rows, 2 * blk), 1)
    in_prev = (kj < blk) & (kj >= qi)
    in_cur = (kj >= blk) & (kj - blk <= qi)
    scale = HEAD_DIM ** -0.5

    def attend(dil, mode, start, kp_ref, kp_start, kc_ref, vp_ref, vc_ref, mask):
        sel = _rows(start, dil)
        psel = _rows(kp_start, dil)
        q = jnp.concatenate([q_ref[sel, :] for q_ref in q_refs], axis=0).astype(BF16)
        k = jnp.concatenate([kp_ref[psel, :], kc_ref[sel, :]], axis=0).astype(BF16)
        v = jnp.concatenate([vp_ref[psel, :], vc_ref[sel, :]], axis=0).astype(BF16)
        s = lax.dot_general(q, k, (((1,), (1,)), ((), ())),
                            preferred_element_type=F32) * scale
        s = jnp.where(mask, s, -jnp.inf)
        m_blk = jnp.max(s, axis=-1, keepdims=True)
        if mode == "init":
            m_new = jnp.broadcast_to(m_blk, (rows, HEAD_DIM))
        else:
            m_old = jnp.concatenate([m_ref.at[h][sel, :] for h in range(n_q)], axis=0)
            l_old = jnp.concatenate([l_ref.at[h][sel, :] for h in range(n_q)], axis=0)
            acc_old = jnp.concatenate([acc_ref.at[h][sel, :] for h in range(n_q)], axis=0)
            m_new = jnp.maximum(m_old, m_blk)
            alpha = jnp.exp(m_old - m_new)
        p = jnp.exp(s - jnp.concatenate([m_new, m_new], axis=1))
        l_new = jnp.sum(p, axis=-1, keepdims=True)
        acc_new = jnp.dot(p.astype(BF16), v, preferred_element_type=F32)
        if mode == "init":
            l_new = jnp.broadcast_to(l_new, (rows, HEAD_DIM))
        else:
            l_new = alpha * l_old + l_new
            acc_new = alpha * acc_old + acc_new
        for h in range(n_q):
            r = slice(h * blk, (h + 1) * blk)
            if mode == "final":
                o_ref[sel, h * HEAD_DIM:(h + 1) * HEAD_DIM] = (
                    acc_new[r] / l_new[r]).astype(o_ref.dtype)
            else:
                m_ref.at[h][sel, :] = m_new[r]
                l_ref.at[h][sel, :] = l_new[r]
                acc_ref.at[h][sel, :] = acc_new[r]

    modes = {16: "init", 4: "update", 1: "final"}
    for branch, (window, dil) in enumerate(A_BRANCHES):
        kc_ref, kp_ref, vc_ref, vp_ref = kv_refs[4 * branch:4 * branch + 4]
        span = dil * blk
        n_blocks = A_TILE // span
        first_mask = in_cur | (in_prev & jnp.logical_not(first_tile))

        def first_block(r, carry, dil=dil, kc_ref=kc_ref, kp_ref=kp_ref, vc_ref=vc_ref,
                        vp_ref=vp_ref, first_mask=first_mask):
            attend(dil, modes[dil], r, kp_ref, r, kc_ref, vp_ref, vc_ref, first_mask)
            return carry

        def later_block(i, carry, dil=dil, span=span, kc_ref=kc_ref, vc_ref=vc_ref):
            start = i % dil + span * (i // dil + 1)
            attend(dil, modes[dil], start, kc_ref, start - span, kc_ref, vc_ref, vc_ref,
                   in_cur | in_prev)
            return carry

        lax.fori_loop(0, dil, first_block, 0)
        if n_blocks > 1:
            lax.fori_loop(0, dil * (n_blocks - 1), later_block, 0)


def _dilated_attention(qkv, b, s):
    c = qkv.shape[-1]
    n_kv = (c // HEAD_DIM) // (A_Q_PER_KV + 2 * len(A_BRANCHES))
    n_qh = n_kv * A_Q_PER_KV
    d = n_qh * HEAD_DIM
    qkv = qkv.reshape(b, s, c)
    tiles = s // A_TILE

    def q_spec(h):
        return pl.BlockSpec((None, A_TILE, HEAD_DIM),
                            lambda bi, t, g: (bi, t, g * A_Q_PER_KV + h))

    def kv_specs(dil, is_v):
        col = n_qh + (2 * A_BRANCH_COLUMN[dil] + is_v) * n_kv
        ratio = A_TILE // (dil * A_BLOCK)
        cur = pl.BlockSpec((None, A_TILE, HEAD_DIM), lambda bi, t, g: (bi, t, col + g))
        prev = pl.BlockSpec((None, dil * A_BLOCK, HEAD_DIM),
                            lambda bi, t, g: (bi, jnp.maximum(t * ratio - 1, 0), col + g))
        return [cur, prev]

    in_specs = [q_spec(h) for h in range(A_Q_PER_KV)]
    for _, dil in A_BRANCHES:
        in_specs += kv_specs(dil, 0) + kv_specs(dil, 1)
    state = pltpu.VMEM((A_Q_PER_KV, A_TILE, HEAD_DIM), F32)
    out = pl.pallas_call(
        _dilated_attention_kernel,
        grid=(b, tiles, n_kv),
        in_specs=in_specs,
        out_specs=pl.BlockSpec((None, A_TILE, A_Q_PER_KV * HEAD_DIM),
                               lambda bi, t, g: (bi, t, g)),
        out_shape=jax.ShapeDtypeStruct((b, s, d), BF16),
        scratch_shapes=[state, state, state],
        compiler_params=_params(("parallel", "parallel", "parallel"), 56),
        name="dilated_attention",
    )(*([qkv] * len(in_specs)))
    return out.reshape(b * s, d)


def _diff_attention_kernel(q_ref, k_ref, v_ref, lam_ref, g_ref, o_ref,
                           vt_ref, acc_ref, *, tq, lambda_init):
    qi = pl.program_id(2)
    n_chunks = v_ref.shape[1] // tq
    exp2_scale = HEAD_DIM ** -0.5 * math.log2(math.e)

    @pl.when(qi == 0)
    def _():
        def transpose_chunk(kc, carry):
            start = pl.multiple_of(kc * tq, tq)
            vt_ref[:, pl.ds(start, tq)] = v_ref[0, pl.ds(start, tq), :].T
            return carry
        lax.fori_loop(0, n_chunks, transpose_chunk, 0)

    acc_ref[...] = jnp.zeros(acc_ref.shape, F32)
    q = q_ref[0]

    def attend(kc, state, masked):
        start = pl.multiple_of(kc * tq, tq)
        k = k_ref[0, pl.ds(start, tq), :]
        vt = vt_ref[:, pl.ds(start, tq)]
        scores = []
        for c in range(2):
            lanes = slice(c * HEAD_DIM, (c + 1) * HEAD_DIM)
            scores.append(lax.dot_general(k[:, lanes], q[:, lanes], (((1,), (1,)), ((), ())),
                                          preferred_element_type=F32))
        new_state = []
        for c in range(2):
            m_old, l_old = state[c]
            st = scores[c]
            if masked:
                key = lax.broadcasted_iota(jnp.int32, st.shape, 0)
                query = lax.broadcasted_iota(jnp.int32, st.shape, 1)
                st = jnp.where(key <= query, st, -jnp.inf)
            m_new = jnp.maximum(m_old, jnp.max(st, axis=0, keepdims=True))
            alpha = jnp.exp2((m_old - m_new) * exp2_scale)
            pt = jnp.exp2((st - m_new) * exp2_scale)
            l_new = alpha * l_old + jnp.sum(pt, axis=0, keepdims=True)
            acc_ref[c] = alpha * acc_ref[c] + jnp.dot(vt, pt.astype(BF16),
                                                      preferred_element_type=F32)
            new_state.append((m_new, l_new))
        return tuple(new_state)

    init = ((jnp.full((1, tq), -jnp.inf, F32), jnp.zeros((1, tq), F32)),) * 2
    state = lax.fori_loop(0, qi, lambda kc, st: attend(kc, st, False), init)
    (_, l0), (_, l1) = attend(qi, state, True)

    lp = lam_ref[...]
    lam = (jnp.exp(jnp.sum(lp[0:1] * lp[1:2], axis=-1, keepdims=True))
           - jnp.exp(jnp.sum(lp[2:3] * lp[3:4], axis=-1, keepdims=True)) + lambda_init)
    ot = acc_ref[0] * (1.0 / l0) - lam * (acc_ref[1] * (1.0 / l1))
    o = _rms_normalize(ot.T, g_ref[...], SUBLN_EPS) * (1.0 - lambda_init)
    o_ref[0] = o.astype(o_ref.dtype)


def _diff_attention(qkv, lam_params, subln_g, wlayer, lambda_init, b, s, tq):
    d = qkv.shape[-1] // 3
    hw = 2 * HEAD_DIM
    n_heads = d // hw
    qkv = qkv.reshape(b, s, 3 * d)
    out = pl.pallas_call(
        functools.partial(_diff_attention_kernel, tq=tq, lambda_init=lambda_init),
        grid=(b, n_heads, s // tq),
        in_specs=[
            pl.BlockSpec((1, tq, hw), lambda bi, h, i: (bi, i, h)),
            pl.BlockSpec((1, s, hw), lambda bi, h, i: (bi, 0, n_heads + h)),
            pl.BlockSpec((1, s, hw), lambda bi, h, i: (bi, 0, 2 * n_heads + h)),
            pl.BlockSpec((None, 4, HEAD_DIM), lambda bi, h, i: (wlayer, 0, 0)),
            pl.BlockSpec((None, 1, hw), lambda bi, h, i: (wlayer, 0, 0)),
        ],
        out_specs=pl.BlockSpec((1, tq, hw), lambda bi, h, i: (bi, i, h)),
        out_shape=jax.ShapeDtypeStruct((b, s, d), BF16),
        scratch_shapes=[pltpu.VMEM((hw, s), BF16), pltpu.VMEM((2, hw, tq), F32)],
        compiler_params=_params(("parallel", "parallel", "arbitrary"), 40),
        name="diff_attention",
    )(qkv, qkv, qkv, lam_params, subln_g)
    return out.reshape(b * s, d)


def _conv_proj_kernel(bg_ref, cg_ref, u_ref, cgh_ref, uh_ref, cw_ref, w_ref, h_ref, o_ref,
                      z_ref, *, tiles_per_seq):
    tm = bg_ref.shape[0]
    halo = cgh_ref.shape[0]
    seq_start = pl.program_id(0) % tiles_per_seq == 0
    zh = cgh_ref[...] * uh_ref[...]
    z_ref[0:halo, :] = jnp.where(seq_start, 0.0, zh)
    z_ref[halo:, :] = cg_ref[...] * u_ref[...]
    cw = cw_ref[...]
    conv = (cw[0:1] * z_ref[pl.ds(halo - 2, tm), :]
            + cw[1:2] * z_ref[pl.ds(halo - 1, tm), :]
            + cw[2:3] * z_ref[pl.ds(halo, tm), :])
    y = (bg_ref[...] * conv).astype(BF16)
    o_ref[...] = h_ref[...] + jnp.dot(y, w_ref[...], preferred_element_type=F32)


def _conv_proj(proj, conv_w, w_out, wlayer, h, s, tm):
    m, d = h.shape
    halo = 8
    ratio = tm // halo
    return pl.pallas_call(
        functools.partial(_conv_proj_kernel, tiles_per_seq=s // tm),
        grid=(m // tm,),
        in_specs=[
            pl.BlockSpec((tm, d), lambda i: (i, 0)),
            pl.BlockSpec((tm, d), lambda i: (i, 1)),
            pl.BlockSpec((tm, d), lambda i: (i, 2)),
            pl.BlockSpec((halo, d), lambda i: (jnp.maximum(i * ratio - 1, 0), 1)),
            pl.BlockSpec((halo, d), lambda i: (jnp.maximum(i * ratio - 1, 0), 2)),
            pl.BlockSpec((None, 3, d), lambda i: (wlayer, 0, 0)),
            pl.BlockSpec((None, d, d), lambda i: (wlayer, 0, 0)),
            pl.BlockSpec((tm, d), lambda i: (i, 0)),
        ],
        out_specs=pl.BlockSpec((tm, d), lambda i: (i, 0)),
        out_shape=jax.ShapeDtypeStruct((m, d), F32),
        scratch_shapes=[pltpu.VMEM((tm + halo, d), F32)],
        compiler_params=_params(("parallel",), 56),
        name="conv_proj",
    )(proj, proj, proj, proj, proj, conv_w, w_out, h)


def kernel(x, p, norm_ffn1, w_ffn1_in, w_ffn1_out, norm_mix, a_w_qkv, a_w_o, b_w_qkv, b_w_o,
           b_lambda, b_subln, c_w_in, c_conv_w, c_w_out, norm_ffn2, w_ffn2_in, w_ffn2_out,
           norm_ple, w_ple_gate, b_ple_gate, w_ple_proj, norm_f):
    b, s, d = x.shape
    depth = p.shape[0]
    m = b * s
    h = x.reshape(m, d)
    p = p.reshape(depth, m, p.shape[-1])

    def row(t):
        return t.reshape(t.shape[0], 1, t.shape[-1])

    norm_ffn1, norm_mix, norm_ffn2, norm_ple, b_ple_gate, b_subln = (
        row(t) for t in (norm_ffn1, norm_mix, norm_ffn2, norm_ple, b_ple_gate, b_subln))
    norm_f = norm_f.reshape(1, d)
    (w_ffn1_in, w_ffn1_out, a_w_qkv, a_w_o, b_w_qkv, b_w_o, c_w_in, c_w_out,
     w_ffn2_in, w_ffn2_out, w_ple_gate, w_ple_proj) = (
        t.astype(BF16) for t in (w_ffn1_in, w_ffn1_out, a_w_qkv, a_w_o, b_w_qkv, b_w_o,
                                 c_w_in, c_w_out, w_ffn2_in, w_ffn2_out, w_ple_gate,
                                 w_ple_proj))

    for i in range(depth):
        h = _ffn(h, norm_ffn1, w_ffn1_in, w_ffn1_out, i, 512, 512)
        kind, j = i % N_MIXERS, i // N_MIXERS
        if kind == 0:
            qkv = _norm_matmul(h, norm_mix, a_w_qkv, i, j, F32, 1024, 1024)
            mix = _dilated_attention(qkv, b, s)
            h = _proj_residual(mix, a_w_o, j, h, 1024, 1024)
        elif kind == 1:
            lambda_init = 0.8 - 0.6 * math.exp(-0.3 * i)
            qkv = _norm_matmul(h, norm_mix, b_w_qkv, i, j, BF16, 1024, 1024)
            mix = _diff_attention(qkv, b_lambda, b_subln, j, lambda_init, b, s, 512)
            h = _proj_residual(mix, b_w_o, j, h, 1024, 1024)
        else:
            proj = _norm_matmul(h, norm_mix, c_w_in, i, j, F32, 1024, 1024)
            h = _conv_proj(proj, c_conv_w, c_w_out, j, h, s, 256)
        h = _ffn(h, norm_ffn2, w_ffn2_in, w_ffn2_out, i, 512, 512)
        h = _ple(h, norm_ple, w_ple_gate, b_ple_gate, p, w_ple_proj, norm_f, i,
                 i == depth - 1, 512)
    return h.reshape(b, s, d)
```

```python
import functools
import math

import jax
import jax.numpy as jnp
from jax import lax
from jax.experimental import pallas as pl
from jax.experimental.pallas import tpu as pltpu

BF16 = jnp.bfloat16
F32 = jnp.float32

HEAD_DIM = 128
A_Q_PER_KV = 4
A_BRANCHES = ((2048, 16), (512, 4), (128, 1))
A_BRANCH_COLUMN = {1: 0, 4: 1, 16: 2}
A_BLOCK = 128
A_TILE = 2048
RMS_EPS = 1e-6
SUBLN_EPS = 1e-5
N_MIXERS = 3

V7X_VMEM_BYTES = 64 * 1024 * 1024
MIB = 1024 * 1024


def _params(semantics, vmem_mib):
    return pltpu.CompilerParams(dimension_semantics=semantics,
                                vmem_limit_bytes=vmem_mib * MIB)


def _rms_normalize(x, gain, eps):
    ms = jnp.mean(x * x, axis=-1, keepdims=True)
    return x * lax.rsqrt(ms + eps) * gain


def _sigmoid(x):
    return 1.0 / (1.0 + jnp.exp(-x))


def _ffn_kernel(h_ref, g_ref, wg_ref, wu_ref, wo_ref, o_ref, hn_ref):
    @pl.when(pl.program_id(1) == 0)
    def _():
        h = h_ref[...]
        hn_ref[...] = _rms_normalize(h, g_ref[...], RMS_EPS).astype(BF16)
        o_ref[...] = h

    hn = hn_ref[...]
    gate = jnp.dot(hn, wg_ref[...], preferred_element_type=F32)
    up = jnp.dot(hn, wu_ref[...], preferred_element_type=F32)
    act = (0.5 * gate * _sigmoid(gate)) * up
    o_ref[...] += jnp.dot(act.astype(BF16), wo_ref[...], preferred_element_type=F32)


def _ffn(h, gain, w_in, w_out, layer, tm, tf):
    m, d = h.shape
    f = w_out.shape[1]
    nf = f // tf
    return pl.pallas_call(
        _ffn_kernel,
        grid=(m // tm, nf),
        in_specs=[
            pl.BlockSpec((tm, d), lambda i, j: (i, 0)),
            pl.BlockSpec((None, 1, d), lambda i, j: (layer, 0, 0)),
            pl.BlockSpec((None, d, tf), lambda i, j: (layer, 0, j)),
            pl.BlockSpec((None, d, tf), lambda i, j: (layer, 0, nf + j)),
            pl.BlockSpec((None, tf, d), lambda i, j: (layer, j, 0)),
        ],
        out_specs=pl.BlockSpec((tm, d), lambda i, j: (i, 0)),
        out_shape=jax.ShapeDtypeStruct((m, d), F32),
        scratch_shapes=[pltpu.VMEM((tm, d), BF16)],
        compiler_params=_params(("parallel", "arbitrary"), 56),
        name="ffn",
    )(h, gain, w_in, w_in, w_out)


def _norm_matmul_kernel(h_ref, g_ref, w_ref, o_ref, hn_ref):
    @pl.when(pl.program_id(1) == 0)
    def _():
        hn_ref[...] = _rms_normalize(h_ref[...], g_ref[...], RMS_EPS).astype(BF16)

    o_ref[...] = jnp.dot(hn_ref[...], w_ref[...],
                         preferred_element_type=F32).astype(o_ref.dtype)


def _norm_matmul(h, gain, w, layer, wlayer, out_dtype, tm, tn):
    m, d = h.shape
    n = w.shape[2]
    return pl.pallas_call(
        _norm_matmul_kernel,
        grid=(m // tm, n // tn),
        in_specs=[
            pl.BlockSpec((tm, d), lambda i, j: (i, 0)),
            pl.BlockSpec((None, 1, d), lambda i, j: (layer, 0, 0)),
            pl.BlockSpec((None, d, tn), lambda i, j: (wlayer, 0, j)),
        ],
        out_specs=pl.BlockSpec((tm, tn), lambda i, j: (i, j)),
        out_shape=jax.ShapeDtypeStruct((m, n), out_dtype),
        scratch_shapes=[pltpu.VMEM((tm, d), BF16)],
        compiler_params=_params(("parallel", "arbitrary"), 48),
        name="norm_matmul",
    )(h, gain, w)


def _proj_residual_kernel(x_ref, w_ref, h_ref, o_ref):
    o_ref[...] = h_ref[...] + jnp.dot(x_ref[...], w_ref[...],
                                      preferred_element_type=F32)


def _proj_residual(x, w, wlayer, h, tm, tn):
    m, k = x.shape
    n = w.shape[2]
    return pl.pallas_call(
        _proj_residual_kernel,
        grid=(m // tm, n // tn),
        in_specs=[
            pl.BlockSpec((tm, k), lambda i, j: (i, 0)),
            pl.BlockSpec((None, k, tn), lambda i, j: (wlayer, 0, j)),
            pl.BlockSpec((tm, tn), lambda i, j: (i, j)),
        ],
        out_specs=pl.BlockSpec((tm, tn), lambda i, j: (i, j)),
        out_shape=jax.ShapeDtypeStruct((m, n), F32),
        compiler_params=_params(("parallel", "arbitrary"), 48),
        name="proj_residual",
    )(x, w, h)


def _ple_kernel(h_ref, g_ref, wg_ref, b_ref, p_ref, wp_ref, gf_ref, o_ref, *, final_norm):
    h = h_ref[...]
    hn = _rms_normalize(h, g_ref[...], RMS_EPS).astype(BF16)
    gate = _sigmoid(jnp.dot(hn, wg_ref[...], preferred_element_type=F32) + b_ref[...])
    proj = jnp.dot(p_ref[...].astype(BF16), wp_ref[...], preferred_element_type=F32)
    out = h + gate * proj
    if final_norm:
        out = _rms_normalize(out, gf_ref[...], RMS_EPS)
    o_ref[...] = out


def _ple(h, gain, w_gate, b_gate, p, w_proj, gain_f, layer, final_norm, tm):
    m, d = h.shape
    e = p.shape[2]
    nblk = m // tm
    return pl.pallas_call(
        functools.partial(_ple_kernel, final_norm=final_norm),
        grid=(nblk,),
        in_specs=[
            pl.BlockSpec((tm, d), lambda i: (i, 0)),
            pl.BlockSpec((None, 1, d), lambda i: (layer, 0, 0)),
            pl.BlockSpec((None, d, d), lambda i: (layer, 0, 0)),
            pl.BlockSpec((None, 1, d), lambda i: (layer, 0, 0)),
            pl.BlockSpec((None, tm, e), lambda i: (layer, i, 0)),
            pl.BlockSpec((None, e, d), lambda i: (layer, 0, 0)),
            pl.BlockSpec((1, d), lambda i: (0, 0)),
        ],
        out_specs=pl.BlockSpec((tm, d), lambda i: (i, 0)),
        out_shape=jax.ShapeDtypeStruct((m, d), F32),
        compiler_params=_params(("parallel",), 56),
        name="ple",
    )(h, gain, w_gate, b_gate, p, w_proj, gain_f)


def _rows(start, dil):
    if dil == 1:
        return pl.ds(pl.multiple_of(start, A_BLOCK), A_BLOCK)
    return pl.ds(start, A_BLOCK, stride=dil)


def _dilated_attention_kernel(*refs):
    n_q = A_Q_PER_KV
    q_refs = refs[:n_q]
    kv_refs = refs[n_q:n_q + 12]
    o_ref = refs[n_q + 12]
    m_ref, l_ref, acc_ref = refs[n_q + 13:]
    blk = A_BLOCK
    first_tile = pl.program_id(1) == 0
    rows = n_q * blk
    qi = lax.broadcasted_iota(jnp.int32, (rows, 2 * blk), 0) % blk
    kj = lax.broadcasted_iota(jnp.int32, (rows, 2 * blk), 1)
    in_prev = (kj < blk) & (kj >= qi)
    in_cur = (kj >= blk) & (kj - blk <= qi)
    scale = HEAD_DIM ** -0.5

    def attend(dil, mode, start, kp_ref, kp_start, kc_ref, vp_ref, vc_ref, mask):
        sel = _rows(start, dil)
        psel = _rows(kp_start, dil)
        q = jnp.concatenate([q_ref[sel, :] for q_ref in q_refs], axis=0).astype(BF16)
        k = jnp.concatenate([kp_ref[psel, :], kc_ref[sel, :]], axis=0).astype(BF16)
        v = jnp.concatenate([vp_ref[psel, :], vc_ref[sel, :]], axis=0).astype(BF16)
        s = lax.dot_general(q, k, (((1,), (1,)), ((), ())),
                            preferred_element_type=F32) * scale
        s = jnp.where(mask, s, -jnp.inf)
        m_blk = jnp.max(s, axis=-1, keepdims=True)
        if mode == "init":
            m_new = jnp.broadcast_to(m_blk, (rows, HEAD_DIM))
        else:
            m_old = jnp.concatenate([m_ref.at[h][sel, :] for h in range(n_q)], axis=0)
            l_old = jnp.concatenate([l_ref.at[h][sel, :] for h in range(n_q)], axis=0)
            acc_old = jnp.concatenate([acc_ref.at[h][sel, :] for h in range(n_q)], axis=0)
            m_new = jnp.maximum(m_old, m_blk)
            alpha = jnp.exp(m_old - m_new)
        p = jnp.exp(s - jnp.concatenate([m_new, m_new], axis=1))
        l_new = jnp.sum(p, axis=-1, keepdims=True)
        acc_new = jnp.dot(p.astype(BF16), v, preferred_element_type=F32)
        if mode == "init":
            l_new = jnp.broadcast_to(l_new, (rows, HEAD_DIM))
        else:
            l_new = alpha * l_old + l_new
            acc_new = alpha * acc_old + acc_new
        for h in range(n_q):
            r = slice(h * blk, (h + 1) * blk)
            if mode == "final":
                o_ref[sel, h * HEAD_DIM:(h + 1) * HEAD_DIM] = (
                    acc_new[r] / l_new[r]).astype(o_ref.dtype)
            else:
                m_ref.at[h][sel, :] = m_new[r]
                l_ref.at[h][sel, :] = l_new[r]
                acc_ref.at[h][sel, :] = acc_new[r]

    modes = {16: "init", 4: "update", 1: "final"}
    for branch, (window, dil) in enumerate(A_BRANCHES):
        kc_ref, kp_ref, vc_ref, vp_ref = kv_refs[4 * branch:4 * branch + 4]
        span = dil * blk
        n_blocks = A_TILE // span
        first_mask = in_cur | (in_prev & jnp.logical_not(first_tile))

        def first_block(r, carry, dil=dil, kc_ref=kc_ref, kp_ref=kp_ref, vc_ref=vc_ref,
                        vp_ref=vp_ref, first_mask=first_mask):
            attend(dil, modes[dil], r, kp_ref, r, kc_ref, vp_ref, vc_ref, first_mask)
            return carry

        def later_block(i, carry, dil=dil, span=span, kc_ref=kc_ref, vc_ref=vc_ref):
            start = i % dil + span * (i // dil + 1)
            attend(dil, modes[dil], start, kc_ref, start - span, kc_ref, vc_ref, vc_ref,
                   in_cur | in_prev)
            return carry

        lax.fori_loop(0, dil, first_block, 0)
        if n_blocks > 1:
            lax.fori_loop(0, dil * (n_blocks - 1), later_block, 0)


def _dilated_attention(qkv, b, s):
    c = qkv.shape[-1]
    n_kv = (c // HEAD_DIM) // (A_Q_PER_KV + 2 * len(A_BRANCHES))
    n_qh = n_kv * A_Q_PER_KV
    d = n_qh * HEAD_DIM
    qkv = qkv.reshape(b, s, c)
    tiles = s // A_TILE

    def q_spec(h):
        return pl.BlockSpec((None, A_TILE, HEAD_DIM),
                            lambda bi, t, g: (bi, t, g * A_Q_PER_KV + h))

    def kv_specs(dil, is_v):
        col = n_qh + (2 * A_BRANCH_COLUMN[dil] + is_v) * n_kv
        ratio = A_TILE // (dil * A_BLOCK)
        cur = pl.BlockSpec((None, A_TILE, HEAD_DIM), lambda bi, t, g: (bi, t, col + g))
        prev = pl.BlockSpec((None, dil * A_BLOCK, HEAD_DIM),
                            lambda bi, t, g: (bi, jnp.maximum(t * ratio - 1, 0), col + g))
        return [cur, prev]

    in_specs = [q_spec(h) for h in range(A_Q_PER_KV)]
    for _, dil in A_BRANCHES:
        in_specs += kv_specs(dil, 0) + kv_specs(dil, 1)
    state = pltpu.VMEM((A_Q_PER_KV, A_TILE, HEAD_DIM), F32)
    out = pl.pallas_call(
        _dilated_attention_kernel,
        grid=(b, tiles, n_kv),
        in_specs=in_specs,
        out_specs=pl.BlockSpec((None, A_TILE, A_Q_PER_KV * HEAD_DIM),
                               lambda bi, t, g: (bi, t, g)),
        out_shape=jax.ShapeDtypeStruct((b, s, d), BF16),
        scratch_shapes=[state, state, state],
        compiler_params=_params(("parallel", "parallel", "parallel"), 56),
        name="dilated_attention",
    )(*([qkv] * len(in_specs)))
    return out.reshape(b * s, d)


def _diff_attention_kernel(q_ref, k_ref, v_ref, lam_ref, g_ref, o_ref,
                           vt_ref, acc_ref, sa_ref, sb_ref, *, tq, lambda_init):
    qi = pl.program_id(2)
    n_chunks = v_ref.shape[1] // tq
    exp2_scale = HEAD_DIM ** -0.5 * math.log2(math.e)

    @pl.when(qi == 0)
    def _():
        def transpose_chunk(kc, carry):
            start = pl.multiple_of(kc * tq, tq)
            vt_ref[:, pl.ds(start, tq)] = v_ref[0, pl.ds(start, tq), :].T
            return carry
        lax.fori_loop(0, n_chunks, transpose_chunk, 0)

    acc_ref[...] = jnp.zeros(acc_ref.shape, F32)
    q = q_ref[0]

    def scores_into(s_ref, kc):
        start = pl.multiple_of(kc * tq, tq)
        k = k_ref[0, pl.ds(start, tq), :]
        for c in range(2):
            lanes = slice(c * HEAD_DIM, (c + 1) * HEAD_DIM)
            s_ref[c] = lax.dot_general(k[:, lanes], q[:, lanes], (((1,), (1,)), ((), ())),
                                       preferred_element_type=F32)

    def attend(s_ref, kc, state, masked):
        start = pl.multiple_of(kc * tq, tq)
        vt = vt_ref[:, pl.ds(start, tq)]
        new_state = []
        for c in range(2):
            m_old, l_old = state[c]
            st = s_ref[c]
            if masked:
                key = lax.broadcasted_iota(jnp.int32, st.shape, 0)
                query = lax.broadcasted_iota(jnp.int32, st.shape, 1)
                st = jnp.where(key <= query, st, -jnp.inf)
            m_new = jnp.maximum(m_old, jnp.max(st, axis=0, keepdims=True))
            alpha = jnp.exp2((m_old - m_new) * exp2_scale)
            pt = jnp.exp2((st - m_new) * exp2_scale)
            l_new = alpha * l_old + jnp.sum(pt, axis=0, keepdims=True)
            acc_ref[c] = alpha * acc_ref[c] + jnp.dot(vt, pt.astype(BF16),
                                                      preferred_element_type=F32)
            new_state.append((m_new, l_new))
        return tuple(new_state)

    def chunk_pair(j, state):
        scores_into(sb_ref, 2 * j + 1)
        state = attend(sa_ref, 2 * j, state, False)
        scores_into(sa_ref, 2 * j + 2)
        return attend(sb_ref, 2 * j + 1, state, False)

    def even_tail(state):
        return attend(sa_ref, qi, state, True)

    def odd_tail(state):
        scores_into(sb_ref, qi)
        state = attend(sa_ref, qi - 1, state, False)
        return attend(sb_ref, qi, state, True)

    init = ((jnp.full((1, tq), -jnp.inf, F32), jnp.zeros((1, tq), F32)),) * 2
    scores_into(sa_ref, 0)
    state = lax.fori_loop(0, qi // 2, chunk_pair, init)
    (_, l0), (_, l1) = lax.cond(qi % 2 == 0, even_tail, odd_tail, state)

    lp = lam_ref[...]
    lam = (jnp.exp(jnp.sum(lp[0:1] * lp[1:2], axis=-1, keepdims=True))
           - jnp.exp(jnp.sum(lp[2:3] * lp[3:4], axis=-1, keepdims=True)) + lambda_init)
    ot = acc_ref[0] * (1.0 / l0) - lam * (acc_ref[1] * (1.0 / l1))
    o = _rms_normalize(ot.T, g_ref[...], SUBLN_EPS) * (1.0 - lambda_init)
    o_ref[0] = o.astype(o_ref.dtype)


def _diff_attention(qkv, lam_params, subln_g, wlayer, lambda_init, b, s, tq):
    d = qkv.shape[-1] // 3
    hw = 2 * HEAD_DIM
    n_heads = d // hw
    qkv = qkv.reshape(b, s, 3 * d)
    out = pl.pallas_call(
        functools.partial(_diff_attention_kernel, tq=tq, lambda_init=lambda_init),
        grid=(b, n_heads, s // tq),
        in_specs=[
            pl.BlockSpec((1, tq, hw), lambda bi, h, i: (bi, i, h)),
            pl.BlockSpec((1, s, hw), lambda bi, h, i: (bi, 0, n_heads + h)),
            pl.BlockSpec((1, s, hw), lambda bi, h, i: (bi, 0, 2 * n_heads + h)),
            pl.BlockSpec((None, 4, HEAD_DIM), lambda bi, h, i: (wlayer, 0, 0)),
            pl.BlockSpec((None, 1, hw), lambda bi, h, i: (wlayer, 0, 0)),
        ],
        out_specs=pl.BlockSpec((1, tq, hw), lambda bi, h, i: (bi, i, h)),
        out_shape=jax.ShapeDtypeStruct((b, s, d), BF16),
        scratch_shapes=[pltpu.VMEM((hw, s), BF16), pltpu.VMEM((2, hw, tq), F32),
                        pltpu.VMEM((2, tq, tq), F32), pltpu.VMEM((2, tq, tq), F32)],
        compiler_params=_params(("parallel", "parallel", "arbitrary"), 40),
        name="diff_attention",
    )(qkv, qkv, qkv, lam_params, subln_g)
    return out.reshape(b * s, d)


def _conv_proj_kernel(bg_ref, cg_ref, u_ref, cgh_ref, uh_ref, cw_ref, w_ref, h_ref, o_ref,
                      z_ref, *, tiles_per_seq):
    tm = bg_ref.shape[0]
    halo = cgh_ref.shape[0]
    seq_start = pl.program_id(0) % tiles_per_seq == 0
    zh = cgh_ref[...] * uh_ref[...]
    z_ref[0:halo, :] = jnp.where(seq_start, 0.0, zh)
    z_ref[halo:, :] = cg_ref[...] * u_ref[...]
    cw = cw_ref[...]
    conv = (cw[0:1] * z_ref[pl.ds(halo - 2, tm), :]
            + cw[1:2] * z_ref[pl.ds(halo - 1, tm), :]
            + cw[2:3] * z_ref[pl.ds(halo, tm), :])
    y = (bg_ref[...] * conv).astype(BF16)
    o_ref[...] = h_ref[...] + jnp.dot(y, w_ref[...], preferred_element_type=F32)


def _conv_proj(proj, conv_w, w_out, wlayer, h, s, tm):
    m, d = h.shape
    halo = 8
    ratio = tm // halo
    return pl.pallas_call(
        functools.partial(_conv_proj_kernel, tiles_per_seq=s // tm),
        grid=(m // tm,),
        in_specs=[
            pl.BlockSpec((tm, d), lambda i: (i, 0)),
            pl.BlockSpec((tm, d), lambda i: (i, 1)),
            pl.BlockSpec((tm, d), lambda i: (i, 2)),
            pl.BlockSpec((halo, d), lambda i: (jnp.maximum(i * ratio - 1, 0), 1)),
            pl.BlockSpec((halo, d), lambda i: (jnp.maximum(i * ratio - 1, 0), 2)),
            pl.BlockSpec((None, 3, d), lambda i: (wlayer, 0, 0)),
            pl.BlockSpec((None, d, d), lambda i: (wlayer, 0, 0)),
            pl.BlockSpec((tm, d), lambda i: (i, 0)),
        ],
        out_specs=pl.BlockSpec((tm, d), lambda i: (i, 0)),
        out_shape=jax.ShapeDtypeStruct((m, d), F32),
        scratch_shapes=[pltpu.VMEM((tm + halo, d), F32)],
        compiler_params=_params(("parallel",), 56),
        name="conv_proj",
    )(proj, proj, proj, proj, proj, conv_w, w_out, h)


def kernel(x, p, norm_ffn1, w_ffn1_in, w_ffn1_out, norm_mix, a_w_qkv, a_w_o, b_w_qkv, b_w_o,
           b_lambda, b_subln, c_w_in, c_conv_w, c_w_out, norm_ffn2, w_ffn2_in, w_ffn2_out,
           norm_ple, w_ple_gate, b_ple_gate, w_ple_proj, norm_f):
    b, s, d = x.shape
    depth = p.shape[0]
    m = b * s
    h = x.reshape(m, d)
    p = p.reshape(depth, m, p.shape[-1])

    def row(t):
        return t.reshape(t.shape[0], 1, t.shape[-1])

    norm_ffn1, norm_mix, norm_ffn2, norm_ple, b_ple_gate, b_subln = (
        row(t) for t in (norm_ffn1, norm_mix, norm_ffn2, norm_ple, b_ple_gate, b_subln))
    norm_f = norm_f.reshape(1, d)
    (w_ffn1_in, w_ffn1_out, a_w_qkv, a_w_o, b_w_qkv, b_w_o, c_w_in, c_w_out,
     w_ffn2_in, w_ffn2_out, w_ple_gate, w_ple_proj) = (
        t.astype(BF16) for t in (w_ffn1_in, w_ffn1_out, a_w_qkv, a_w_o, b_w_qkv, b_w_o,
                                 c_w_in, c_w_out, w_ffn2_in, w_ffn2_out, w_ple_gate,
                                 w_ple_proj))

    for i in range(depth):
        h = _ffn(h, norm_ffn1, w_ffn1_in, w_ffn1_out, i, 1024, 512)
        kind, j = i % N_MIXERS, i // N_MIXERS
        if kind == 0:
            qkv = _norm_matmul(h, norm_mix, a_w_qkv, i, j, F32, 1024, 1024)
            mix = _dilated_attention(qkv, b, s)
            h = _proj_residual(mix, a_w_o, j, h, 1024, 1024)
        elif kind == 1:
            lambda_init = 0.8 - 0.6 * math.exp(-0.3 * i)
            qkv = _norm_matmul(h, norm_mix, b_w_qkv, i, j, BF16, 1024, 1024)
            mix = _diff_attention(qkv, b_lambda, b_subln, j, lambda_init, b, s, 512)
            h = _proj_residual(mix, b_w_o, j, h, 1024, 1024)
        else:
            proj = _norm_matmul(h, norm_mix, c_w_in, i, j, F32, 1024, 1024)
            h = _conv_proj(proj, c_conv_w, c_w_out, j, h, s, 256)
        h = _ffn(h, norm_ffn2, w_ffn2_in, w_ffn2_out, i, 1024, 512)
        h = _ple(h, norm_ple, w_ple_gate, b_ple_gate, p, w_ple_proj, norm_f, i,
                 i == depth - 1, 512)
    return h.reshape(b, s, d)
```

```python
import functools
import math

import jax
import jax.numpy as jnp
from jax import lax
from jax.experimental import pallas as pl
from jax.experimental.pallas import tpu as pltpu

BF16 = jnp.bfloat16
F32 = jnp.float32

HEAD_DIM = 128
A_Q_PER_KV = 4
A_BRANCHES = ((2048, 16), (512, 4), (128, 1))
A_BRANCH_COLUMN = {1: 0, 4: 1, 16: 2}
A_BLOCK = 128
A_TILE = 2048
RMS_EPS = 1e-6
SUBLN_EPS = 1e-5
N_MIXERS = 3

V7X_VMEM_BYTES = 64 * 1024 * 1024
MIB = 1024 * 1024


def _params(semantics, vmem_mib):
    return pltpu.CompilerParams(dimension_semantics=semantics,
                                vmem_limit_bytes=vmem_mib * MIB)


def _rms_normalize(x, gain, eps):
    ms = jnp.mean(x * x, axis=-1, keepdims=True)
    return x * lax.rsqrt(ms + eps) * gain


def _sigmoid(x):
    return 1.0 / (1.0 + jnp.exp(-x))


def _ffn_kernel(h_ref, g_ref, wg_ref, wu_ref, wo_ref, o_ref, hn_ref):
    @pl.when(pl.program_id(1) == 0)
    def _():
        h = h_ref[...]
        hn_ref[...] = _rms_normalize(h, g_ref[...], RMS_EPS).astype(BF16)
        o_ref[...] = h

    hn = hn_ref[...]
    gate = jnp.dot(hn, wg_ref[...], preferred_element_type=F32)
    up = jnp.dot(hn, wu_ref[...], preferred_element_type=F32)
    act = (0.5 * gate * _sigmoid(gate)) * up
    o_ref[...] += jnp.dot(act.astype(BF16), wo_ref[...], preferred_element_type=F32)


def _ffn(h, gain, w_in, w_out, layer, tm, tf):
    m, d = h.shape
    f = w_out.shape[1]
    nf = f // tf
    return pl.pallas_call(
        _ffn_kernel,
        grid=(m // tm, nf),
        in_specs=[
            pl.BlockSpec((tm, d), lambda i, j: (i, 0)),
            pl.BlockSpec((None, 1, d), lambda i, j: (layer, 0, 0)),
            pl.BlockSpec((None, d, tf), lambda i, j: (layer, 0, j)),
            pl.BlockSpec((None, d, tf), lambda i, j: (layer, 0, nf + j)),
            pl.BlockSpec((None, tf, d), lambda i, j: (layer, j, 0)),
        ],
        out_specs=pl.BlockSpec((tm, d), lambda i, j: (i, 0)),
        out_shape=jax.ShapeDtypeStruct((m, d), F32),
        scratch_shapes=[pltpu.VMEM((tm, d), BF16)],
        compiler_params=_params(("parallel", "arbitrary"), 56),
        name="ffn",
    )(h, gain, w_in, w_in, w_out)


def _norm_matmul_kernel(h_ref, g_ref, w_ref, o_ref, hn_ref):
    @pl.when(pl.program_id(1) == 0)
    def _():
        hn_ref[...] = _rms_normalize(h_ref[...], g_ref[...], RMS_EPS).astype(BF16)

    o_ref[...] = jnp.dot(hn_ref[...], w_ref[...],
                         preferred_element_type=F32).astype(o_ref.dtype)


def _norm_matmul(h, gain, w, layer, wlayer, out_dtype, tm, tn):
    m, d = h.shape
    n = w.shape[2]
    return pl.pallas_call(
        _norm_matmul_kernel,
        grid=(m // tm, n // tn),
        in_specs=[
            pl.BlockSpec((tm, d), lambda i, j: (i, 0)),
            pl.BlockSpec((None, 1, d), lambda i, j: (layer, 0, 0)),
            pl.BlockSpec((None, d, tn), lambda i, j: (wlayer, 0, j)),
        ],
        out_specs=pl.BlockSpec((tm, tn), lambda i, j: (i, j)),
        out_shape=jax.ShapeDtypeStruct((m, n), out_dtype),
        scratch_shapes=[pltpu.VMEM((tm, d), BF16)],
        compiler_params=_params(("parallel", "arbitrary"), 48),
        name="norm_matmul",
    )(h, gain, w)


def _proj_residual_kernel(x_ref, w_ref, h_ref, o_ref):
    o_ref[...] = h_ref[...] + jnp.dot(x_ref[...], w_ref[...],
                                      preferred_element_type=F32)


def _proj_residual(x, w, wlayer, h, tm, tn):
    m, k = x.shape
    n = w.shape[2]
    return pl.pallas_call(
        _proj_residual_kernel,
        grid=(m // tm, n // tn),
        in_specs=[
            pl.BlockSpec((tm, k), lambda i, j: (i, 0)),
            pl.BlockSpec((None, k, tn), lambda i, j: (wlayer, 0, j)),
            pl.BlockSpec((tm, tn), lambda i, j: (i, j)),
        ],
        out_specs=pl.BlockSpec((tm, tn), lambda i, j: (i, j)),
        out_shape=jax.ShapeDtypeStruct((m, n), F32),
        compiler_params=_params(("parallel", "arbitrary"), 48),
        name="proj_residual",
    )(x, w, h)


def _ple_kernel(h_ref, g_ref, wg_ref, b_ref, p_ref, wp_ref, gf_ref, o_ref, *, final_norm):
    h = h_ref[...]
    hn = _rms_normalize(h, g_ref[...], RMS_EPS).astype(BF16)
    gate = _sigmoid(jnp.dot(hn, wg_ref[...], preferred_element_type=F32) + b_ref[...])
    proj = jnp.dot(p_ref[...].astype(BF16), wp_ref[...], preferred_element_type=F32)
    out = h + gate * proj
    if final_norm:
        out = _rms_normalize(out, gf_ref[...], RMS_EPS)
    o_ref[...] = out


def _ple(h, gain, w_gate, b_gate, p, w_proj, gain_f, layer, final_norm, tm):
    m, d = h.shape
    e = p.shape[2]
    nblk = m // tm
    return pl.pallas_call(
        functools.partial(_ple_kernel, final_norm=final_norm),
        grid=(nblk,),
        in_specs=[
            pl.BlockSpec((tm, d), lambda i: (i, 0)),
            pl.BlockSpec((None, 1, d), lambda i: (layer, 0, 0)),
            pl.BlockSpec((None, d, d), lambda i: (layer, 0, 0)),
            pl.BlockSpec((None, 1, d), lambda i: (layer, 0, 0)),
            pl.BlockSpec((None, tm, e), lambda i: (layer, i, 0)),
            pl.BlockSpec((None, e, d), lambda i: (layer, 0, 0)),
            pl.BlockSpec((1, d), lambda i: (0, 0)),
        ],
        out_specs=pl.BlockSpec((tm, d), lambda i: (i, 0)),
        out_shape=jax.ShapeDtypeStruct((m, d), F32),
        compiler_params=_params(("parallel",), 56),
        name="ple",
    )(h, gain, w_gate, b_gate, p, w_proj, gain_f)


def _rows(start, dil):
    if dil == 1:
        return pl.ds(pl.multiple_of(start, A_BLOCK), A_BLOCK)
    return pl.ds(start, A_BLOCK, stride=dil)


def _dilated_attention_kernel(*refs):
    n_q = A_Q_PER_KV
    q_refs = refs[:n_q]
    kv_refs = refs[n_q:n_q + 12]
    o_ref = refs[n_q + 12]
    m_ref, l_ref, acc_ref, bias_ref = refs[n_q + 13:]
    blk = A_BLOCK
    rows = n_q * blk
    exp2_scale = HEAD_DIM ** -0.5 * math.log2(math.e)

    qi = lax.broadcasted_iota(jnp.int32, (rows, 2 * blk), 0) % blk
    kj = lax.broadcasted_iota(jnp.int32, (rows, 2 * blk), 1)
    band = (kj >= qi) & (kj <= qi + blk)
    bias_ref[0] = jnp.where(band, 0.0, -jnp.inf)
    bias_ref[1] = jnp.where(band & (kj >= blk), 0.0, -jnp.inf)
    first_bias = (pl.program_id(1) == 0).astype(jnp.int32)

    def scores(dil, start, kp_ref, kp_start, kc_ref, vp_ref, vc_ref, bias_idx):
        sel = _rows(start, dil)
        psel = _rows(kp_start, dil)
        q = jnp.concatenate([q_ref[sel, :] for q_ref in q_refs], axis=0).astype(BF16)
        k = jnp.concatenate([kp_ref[psel, :], kc_ref[sel, :]], axis=0).astype(BF16)
        v = jnp.concatenate([vp_ref[psel, :], vc_ref[sel, :]], axis=0).astype(BF16)
        s = lax.dot_general(q, k, (((1,), (1,)), ((), ())),
                            preferred_element_type=F32) + bias_ref[bias_idx]
        return s, v

    def softmax_pv(s, v):
        m_blk = jnp.max(s, axis=-1, keepdims=True)
        p = jnp.exp2((s - m_blk) * exp2_scale)
        l_blk = jnp.sum(p, axis=-1, keepdims=True)
        return m_blk, l_blk, jnp.dot(p.astype(BF16), v, preferred_element_type=F32)

    def merge(dil, mode, start, m_blk, l_blk, pv):
        sel = _rows(start, dil)
        if mode == "init":
            m_new = jnp.broadcast_to(m_blk, (rows, HEAD_DIM))
            l_new = jnp.broadcast_to(l_blk, (rows, HEAD_DIM))
            acc_new = pv
        else:
            m_old = jnp.concatenate([m_ref.at[h][sel, :] for h in range(n_q)], axis=0)
            l_old = jnp.concatenate([l_ref.at[h][sel, :] for h in range(n_q)], axis=0)
            acc_old = jnp.concatenate([acc_ref.at[h][sel, :] for h in range(n_q)], axis=0)
            m_new = jnp.maximum(m_old, m_blk)
            alpha = jnp.exp2((m_old - m_new) * exp2_scale)
            beta = jnp.exp2((m_blk - m_new) * exp2_scale)
            l_new = alpha * l_old + beta * l_blk
            acc_new = alpha * acc_old + beta * pv
        for h in range(n_q):
            r = slice(h * blk, (h + 1) * blk)
            if mode == "final":
                o_ref[sel, h * HEAD_DIM:(h + 1) * HEAD_DIM] = (
                    acc_new[r] / l_new[r]).astype(o_ref.dtype)
            else:
                m_ref.at[h][sel, :] = m_new[r]
                l_ref.at[h][sel, :] = l_new[r]
                acc_ref.at[h][sel, :] = acc_new[r]

    def attend(dil, mode, blocks):
        loaded = [scores(dil, *block) for block in blocks]
        stats = [softmax_pv(s, v) for s, v in loaded]
        for block, stat in zip(blocks, stats):
            merge(dil, mode, block[0], *stat)

    modes = {16: "init", 4: "update", 1: "final"}
    for branch, (_, dil) in enumerate(A_BRANCHES):
        kc_ref, kp_ref, vc_ref, vp_ref = kv_refs[4 * branch:4 * branch + 4]
        span = dil * blk
        n_later = dil * (A_TILE // span - 1)
        mode = modes[dil]

        def first_block(r, kc_ref=kc_ref, kp_ref=kp_ref, vc_ref=vc_ref, vp_ref=vp_ref):
            return (r, kp_ref, r, kc_ref, vp_ref, vc_ref, first_bias)

        def later_block(i, dil=dil, span=span, kc_ref=kc_ref, vc_ref=vc_ref):
            start = i % dil + span * (i // dil + 1)
            return (start, kc_ref, start - span, kc_ref, vc_ref, vc_ref, 0)

        def first_pair(i, carry, dil=dil, mode=mode, first_block=first_block):
            attend(dil, mode, [first_block(2 * i), first_block(2 * i + 1)])
            return carry

        def later_pair(i, carry, offset, dil=dil, mode=mode, later_block=later_block):
            attend(dil, mode, [later_block(2 * i + offset), later_block(2 * i + 1 + offset)])
            return carry

        if dil > 1:
            lax.fori_loop(0, dil // 2, first_pair, 0)
            if n_later:
                lax.fori_loop(0, n_later // 2, functools.partial(later_pair, offset=0), 0)
        else:
            attend(dil, mode, [first_block(0), later_block(0)])
            lax.fori_loop(0, n_later // 2, functools.partial(later_pair, offset=1), 0)


def _dilated_attention(qkv, b, s):
    c = qkv.shape[-1]
    n_kv = (c // HEAD_DIM) // (A_Q_PER_KV + 2 * len(A_BRANCHES))
    n_qh = n_kv * A_Q_PER_KV
    d = n_qh * HEAD_DIM
    qkv = qkv.reshape(b, s, c)
    tiles = s // A_TILE

    def q_spec(h):
        return pl.BlockSpec((None, A_TILE, HEAD_DIM),
                            lambda bi, t, g: (bi, t, g * A_Q_PER_KV + h))

    def kv_specs(dil, is_v):
        col = n_qh + (2 * A_BRANCH_COLUMN[dil] + is_v) * n_kv
        ratio = A_TILE // (dil * A_BLOCK)
        cur = pl.BlockSpec((None, A_TILE, HEAD_DIM), lambda bi, t, g: (bi, t, col + g))
        prev = pl.BlockSpec((None, dil * A_BLOCK, HEAD_DIM),
                            lambda bi, t, g: (bi, jnp.maximum(t * ratio - 1, 0), col + g))
        return [cur, prev]

    in_specs = [q_spec(h) for h in range(A_Q_PER_KV)]
    for _, dil in A_BRANCHES:
        in_specs += kv_specs(dil, 0) + kv_specs(dil, 1)
    state = pltpu.VMEM((A_Q_PER_KV, A_TILE, HEAD_DIM), F32)
    out = pl.pallas_call(
        _dilated_attention_kernel,
        grid=(b, tiles, n_kv),
        in_specs=in_specs,
        out_specs=pl.BlockSpec((None, A_TILE, A_Q_PER_KV * HEAD_DIM),
                               lambda bi, t, g: (bi, t, g)),
        out_shape=jax.ShapeDtypeStruct((b, s, d), BF16),
        scratch_shapes=[state, state, state,
                        pltpu.VMEM((2, A_Q_PER_KV * A_BLOCK, 2 * A_BLOCK), F32)],
        compiler_params=_params(("parallel", "parallel", "parallel"), 56),
        name="dilated_attention",
    )(*([qkv] * len(in_specs)))
    return out.reshape(b * s, d)


def _diff_attention_kernel(q_ref, k_ref, v_ref, lam_ref, g_ref, o_ref,
                           vt_ref, acc_ref, sa_ref, sb_ref, *, tq, lambda_init):
    qi = pl.program_id(2)
    n_chunks = v_ref.shape[1] // tq
    exp2_scale = HEAD_DIM ** -0.5 * math.log2(math.e)

    @pl.when(qi == 0)
    def _():
        def transpose_chunk(kc, carry):
            start = pl.multiple_of(kc * tq, tq)
            vt_ref[:, pl.ds(start, tq)] = v_ref[0, pl.ds(start, tq), :].T
            return carry
        lax.fori_loop(0, n_chunks, transpose_chunk, 0)

    acc_ref[...] = jnp.zeros(acc_ref.shape, F32)
    q = q_ref[0]

    def scores_into(s_ref, kc):
        start = pl.multiple_of(kc * tq, tq)
        k = k_ref[0, pl.ds(start, tq), :]
        for c in range(2):
            lanes = slice(c * HEAD_DIM, (c + 1) * HEAD_DIM)
            s_ref[c] = lax.dot_general(k[:, lanes], q[:, lanes], (((1,), (1,)), ((), ())),
                                       preferred_element_type=F32)

    def attend(s_ref, kc, state, masked):
        start = pl.multiple_of(kc * tq, tq)
        vt = vt_ref[:, pl.ds(start, tq)]
        new_state = []
        for c in range(2):
            m_old, l_old = state[c]
            st = s_ref[c]
            if masked:
                key = lax.broadcasted_iota(jnp.int32, st.shape, 0)
                query = lax.broadcasted_iota(jnp.int32, st.shape, 1)
                st = jnp.where(key <= query, st, -jnp.inf)
            m_new = jnp.maximum(m_old, jnp.max(st, axis=0, keepdims=True))
            alpha = jnp.exp2((m_old - m_new) * exp2_scale)
            pt = jnp.exp2((st - m_new) * exp2_scale)
            l_new = alpha * l_old + jnp.sum(pt, axis=0, keepdims=True)
            acc_ref[c] = alpha * acc_ref[c] + jnp.dot(vt, pt.astype(BF16),
                                                      preferred_element_type=F32)
            new_state.append((m_new, l_new))
        return tuple(new_state)

    def chunk_pair(j, state):
        scores_into(sb_ref, 2 * j + 1)
        state = attend(sa_ref, 2 * j, state, False)
        scores_into(sa_ref, 2 * j + 2)
        return attend(sb_ref, 2 * j + 1, state, False)

    def even_tail(state):
        return attend(sa_ref, qi, state, True)

    def odd_tail(state):
        scores_into(sb_ref, qi)
        state = attend(sa_ref, qi - 1, state, False)
        return attend(sb_ref, qi, state, True)

    init = ((jnp.full((1, tq), -jnp.inf, F32), jnp.zeros((1, tq), F32)),) * 2
    scores_into(sa_ref, 0)
    state = lax.fori_loop(0, qi // 2, chunk_pair, init)
    (_, l0), (_, l1) = lax.cond(qi % 2 == 0, even_tail, odd_tail, state)

    lp = lam_ref[...]
    lam = (jnp.exp(jnp.sum(lp[0:1] * lp[1:2], axis=-1, keepdims=True))
           - jnp.exp(jnp.sum(lp[2:3] * lp[3:4], axis=-1, keepdims=True)) + lambda_init)
    ot = acc_ref[0] * (1.0 / l0) - lam * (acc_ref[1] * (1.0 / l1))
    o = _rms_normalize(ot.T, g_ref[...], SUBLN_EPS) * (1.0 - lambda_init)
    o_ref[0] = o.astype(o_ref.dtype)


def _diff_attention(qkv, lam_params, subln_g, wlayer, lambda_init, b, s, tq):
    d = qkv.shape[-1] // 3
    hw = 2 * HEAD_DIM
    n_heads = d // hw
    qkv = qkv.reshape(b, s, 3 * d)
    out = pl.pallas_call(
        functools.partial(_diff_attention_kernel, tq=tq, lambda_init=lambda_init),
        grid=(b, n_heads, s // tq),
        in_specs=[
            pl.BlockSpec((1, tq, hw), lambda bi, h, i: (bi, i, h)),
            pl.BlockSpec((1, s, hw), lambda bi, h, i: (bi, 0, n_heads + h)),
            pl.BlockSpec((1, s, hw), lambda bi, h, i: (bi, 0, 2 * n_heads + h)),
            pl.BlockSpec((None, 4, HEAD_DIM), lambda bi, h, i: (wlayer, 0, 0)),
            pl.BlockSpec((None, 1, hw), lambda bi, h, i: (wlayer, 0, 0)),
        ],
        out_specs=pl.BlockSpec((1, tq, hw), lambda bi, h, i: (bi, i, h)),
        out_shape=jax.ShapeDtypeStruct((b, s, d), BF16),
        scratch_shapes=[pltpu.VMEM((hw, s), BF16), pltpu.VMEM((2, hw, tq), F32),
                        pltpu.VMEM((2, tq, tq), F32), pltpu.VMEM((2, tq, tq), F32)],
        compiler_params=_params(("parallel", "parallel", "arbitrary"), 40),
        name="diff_attention",
    )(qkv, qkv, qkv, lam_params, subln_g)
    return out.reshape(b * s, d)


def _conv_proj_kernel(bg_ref, cg_ref, u_ref, cgh_ref, uh_ref, cw_ref, w_ref, h_ref, o_ref,
                      z_ref, *, tiles_per_seq):
    tm = bg_ref.shape[0]
    halo = cgh_ref.shape[0]
    seq_start = pl.program_id(0) % tiles_per_seq == 0
    zh = cgh_ref[...] * uh_ref[...]
    z_ref[0:halo, :] = jnp.where(seq_start, 0.0, zh)
    z_ref[halo:, :] = cg_ref[...] * u_ref[...]
    cw = cw_ref[...]
    conv = (cw[0:1] * z_ref[pl.ds(halo - 2, tm), :]
            + cw[1:2] * z_ref[pl.ds(halo - 1, tm), :]
            + cw[2:3] * z_ref[pl.ds(halo, tm), :])
    y = (bg_ref[...] * conv).astype(BF16)
    o_ref[...] = h_ref[...] + jnp.dot(y, w_ref[...], preferred_element_type=F32)


def _conv_proj(proj, conv_w, w_out, wlayer, h, s, tm):
    m, d = h.shape
    halo = 8
    ratio = tm // halo
    return pl.pallas_call(
        functools.partial(_conv_proj_kernel, tiles_per_seq=s // tm),
        grid=(m // tm,),
        in_specs=[
            pl.BlockSpec((tm, d), lambda i: (i, 0)),
            pl.BlockSpec((tm, d), lambda i: (i, 1)),
            pl.BlockSpec((tm, d), lambda i: (i, 2)),
            pl.BlockSpec((halo, d), lambda i: (jnp.maximum(i * ratio - 1, 0), 1)),
            pl.BlockSpec((halo, d), lambda i: (jnp.maximum(i * ratio - 1, 0), 2)),
            pl.BlockSpec((None, 3, d), lambda i: (wlayer, 0, 0)),
            pl.BlockSpec((None, d, d), lambda i: (wlayer, 0, 0)),
            pl.BlockSpec((tm, d), lambda i: (i, 0)),
        ],
        out_specs=pl.BlockSpec((tm, d), lambda i: (i, 0)),
        out_shape=jax.ShapeDtypeStruct((m, d), F32),
        scratch_shapes=[pltpu.VMEM((tm + halo, d), F32)],
        compiler_params=_params(("parallel",), 56),
        name="conv_proj",
    )(proj, proj, proj, proj, proj, conv_w, w_out, h)


def kernel(x, p, norm_ffn1, w_ffn1_in, w_ffn1_out, norm_mix, a_w_qkv, a_w_o, b_w_qkv, b_w_o,
           b_lambda, b_subln, c_w_in, c_conv_w, c_w_out, norm_ffn2, w_ffn2_in, w_ffn2_out,
           norm_ple, w_ple_gate, b_ple_gate, w_ple_proj, norm_f):
    b, s, d = x.shape
    depth = p.shape[0]
    m = b * s
    h = x.reshape(m, d)
    p = p.reshape(depth, m, p.shape[-1])

    def row(t):
        return t.reshape(t.shape[0], 1, t.shape[-1])

    norm_ffn1, norm_mix, norm_ffn2, norm_ple, b_ple_gate, b_subln = (
        row(t) for t in (norm_ffn1, norm_mix, norm_ffn2, norm_ple, b_ple_gate, b_subln))
    norm_f = norm_f.reshape(1, d)
    (w_ffn1_in, w_ffn1_out, a_w_qkv, a_w_o, b_w_qkv, b_w_o, c_w_in, c_w_out,
     w_ffn2_in, w_ffn2_out, w_ple_gate, w_ple_proj) = (
        t.astype(BF16) for t in (w_ffn1_in, w_ffn1_out, a_w_qkv, a_w_o, b_w_qkv, b_w_o,
                                 c_w_in, c_w_out, w_ffn2_in, w_ffn2_out, w_ple_gate,
                                 w_ple_proj))

    for i in range(depth):
        h = _ffn(h, norm_ffn1, w_ffn1_in, w_ffn1_out, i, 1024, 512)
        kind, j = i % N_MIXERS, i // N_MIXERS
        if kind == 0:
            qkv = _norm_matmul(h, norm_mix, a_w_qkv, i, j, F32, 1024, 1024)
            mix = _dilated_attention(qkv, b, s)
            h = _proj_residual(mix, a_w_o, j, h, 1024, 1024)
        elif kind == 1:
            lambda_init = 0.8 - 0.6 * math.exp(-0.3 * i)
            qkv = _norm_matmul(h, norm_mix, b_w_qkv, i, j, BF16, 1024, 1024)
            mix = _diff_attention(qkv, b_lambda, b_subln, j, lambda_init, b, s, 512)
            h = _proj_residual(mix, b_w_o, j, h, 1024, 1024)
        else:
            proj = _norm_matmul(h, norm_mix, c_w_in, i, j, F32, 1024, 1024)
            h = _conv_proj(proj, c_conv_w, c_w_out, j, h, s, 256)
        h = _ffn(h, norm_ffn2, w_ffn2_in, w_ffn2_out, i, 1024, 512)
        h = _ple(h, norm_ple, w_ple_gate, b_ple_gate, p, w_ple_proj, norm_f, i,
                 i == depth - 1, 512)
    return h.reshape(b, s, d)
```

```python
import functools
import math

import jax
import jax.numpy as jnp
from jax import lax
from jax.experimental import pallas as pl
from jax.experimental.pallas import tpu as pltpu

BF16 = jnp.bfloat16
F32 = jnp.float32

HEAD_DIM = 128
A_Q_PER_KV = 4
A_BRANCHES = ((2048, 16), (512, 4), (128, 1))
A_BRANCH_COLUMN = {1: 0, 4: 1, 16: 2}
A_BLOCK = 128
A_TILE = 2048
RMS_EPS = 1e-6
SUBLN_EPS = 1e-5
N_MIXERS = 3

V7X_VMEM_BYTES = 64 * 1024 * 1024
MIB = 1024 * 1024


def _params(semantics, vmem_mib):
    return pltpu.CompilerParams(dimension_semantics=semantics,
                                vmem_limit_bytes=vmem_mib * MIB)


def _rms_normalize(x, gain, eps):
    ms = jnp.mean(x * x, axis=-1, keepdims=True)
    return x * lax.rsqrt(ms + eps) * gain


def _sigmoid(x):
    return 1.0 / (1.0 + jnp.exp(-x))


def _ffn_kernel(h_ref, g_ref, wg_ref, wu_ref, wo_ref, o_ref, hn_ref):
    @pl.when(pl.program_id(1) == 0)
    def _():
        h = h_ref[...]
        hn_ref[...] = _rms_normalize(h, g_ref[...], RMS_EPS).astype(BF16)
        o_ref[...] = h

    hn = hn_ref[...]
    gate = jnp.dot(hn, wg_ref[...], preferred_element_type=F32)
    up = jnp.dot(hn, wu_ref[...], preferred_element_type=F32)
    act = (0.5 * gate * _sigmoid(gate)) * up
    o_ref[...] += jnp.dot(act.astype(BF16), wo_ref[...], preferred_element_type=F32)


def _ffn(h, gain, w_in, w_out, layer, tm, tf):
    m, d = h.shape
    f = w_out.shape[1]
    nf = f // tf
    return pl.pallas_call(
        _ffn_kernel,
        grid=(m // tm, nf),
        in_specs=[
            pl.BlockSpec((tm, d), lambda i, j: (i, 0)),
            pl.BlockSpec((None, 1, d), lambda i, j: (layer, 0, 0)),
            pl.BlockSpec((None, d, tf), lambda i, j: (layer, 0, j)),
            pl.BlockSpec((None, d, tf), lambda i, j: (layer, 0, nf + j)),
            pl.BlockSpec((None, tf, d), lambda i, j: (layer, j, 0)),
        ],
        out_specs=pl.BlockSpec((tm, d), lambda i, j: (i, 0)),
        out_shape=jax.ShapeDtypeStruct((m, d), F32),
        scratch_shapes=[pltpu.VMEM((tm, d), BF16)],
        compiler_params=_params(("parallel", "arbitrary"), 56),
        name="ffn",
    )(h, gain, w_in, w_in, w_out)


def _norm_matmul_kernel(h_ref, g_ref, w_ref, o_ref, *, tn):
    hn = _rms_normalize(h_ref[...], g_ref[...], RMS_EPS).astype(BF16)
    for n0 in range(0, o_ref.shape[1], tn):
        o_ref[:, n0:n0 + tn] = jnp.dot(hn, w_ref[:, n0:n0 + tn],
                                       preferred_element_type=F32).astype(o_ref.dtype)


def _norm_matmul(h, gain, w, layer, wlayer, out_dtype, tm, tn):
    m, d = h.shape
    n = w.shape[2]
    return pl.pallas_call(
        functools.partial(_norm_matmul_kernel, tn=tn),
        grid=(m // tm,),
        in_specs=[
            pl.BlockSpec((tm, d), lambda i: (i, 0)),
            pl.BlockSpec((None, 1, d), lambda i: (layer, 0, 0)),
            pl.BlockSpec((None, d, n), lambda i: (wlayer, 0, 0),
                         pipeline_mode=pl.Buffered(1)),
        ],
        out_specs=pl.BlockSpec((tm, n), lambda i: (i, 0)),
        out_shape=jax.ShapeDtypeStruct((m, n), out_dtype),
        compiler_params=_params(("parallel",), 56),
        name="norm_matmul",
    )(h, gain, w)


def _proj_residual_kernel(x_ref, w_ref, h_ref, o_ref, *, tn):
    x = x_ref[...]
    for n0 in range(0, o_ref.shape[1], tn):
        cols = slice(n0, n0 + tn)
        o_ref[:, cols] = h_ref[:, cols] + jnp.dot(x, w_ref[:, cols],
                                                  preferred_element_type=F32)


def _proj_residual(x, w, wlayer, h, tm, tn):
    m, k = x.shape
    n = w.shape[2]
    return pl.pallas_call(
        functools.partial(_proj_residual_kernel, tn=tn),
        grid=(m // tm,),
        in_specs=[
            pl.BlockSpec((tm, k), lambda i: (i, 0)),
            pl.BlockSpec((None, k, n), lambda i: (wlayer, 0, 0),
                         pipeline_mode=pl.Buffered(1)),
            pl.BlockSpec((tm, n), lambda i: (i, 0)),
        ],
        out_specs=pl.BlockSpec((tm, n), lambda i: (i, 0)),
        out_shape=jax.ShapeDtypeStruct((m, n), F32),
        compiler_params=_params(("parallel",), 48),
        name="proj_residual",
    )(x, w, h)


def _ple_kernel(h_ref, g_ref, wg_ref, b_ref, p_ref, wp_ref, gf_ref, o_ref, *, final_norm):
    h = h_ref[...]
    hn = _rms_normalize(h, g_ref[...], RMS_EPS).astype(BF16)
    gate = _sigmoid(jnp.dot(hn, wg_ref[...], preferred_element_type=F32) + b_ref[...])
    proj = jnp.dot(p_ref[...].astype(BF16), wp_ref[...], preferred_element_type=F32)
    out = h + gate * proj
    if final_norm:
        out = _rms_normalize(out, gf_ref[...], RMS_EPS)
    o_ref[...] = out


def _ple(h, gain, w_gate, b_gate, p, w_proj, gain_f, layer, final_norm, tm):
    m, d = h.shape
    e = p.shape[2]
    nblk = m // tm
    return pl.pallas_call(
        functools.partial(_ple_kernel, final_norm=final_norm),
        grid=(nblk,),
        in_specs=[
            pl.BlockSpec((tm, d), lambda i: (i, 0)),
            pl.BlockSpec((None, 1, d), lambda i: (layer, 0, 0)),
            pl.BlockSpec((None, d, d), lambda i: (layer, 0, 0)),
            pl.BlockSpec((None, 1, d), lambda i: (layer, 0, 0)),
            pl.BlockSpec((None, tm, e), lambda i: (layer, i, 0)),
            pl.BlockSpec((None, e, d), lambda i: (layer, 0, 0)),
            pl.BlockSpec((1, d), lambda i: (0, 0)),
        ],
        out_specs=pl.BlockSpec((tm, d), lambda i: (i, 0)),
        out_shape=jax.ShapeDtypeStruct((m, d), F32),
        compiler_params=_params(("parallel",), 56),
        name="ple",
    )(h, gain, w_gate, b_gate, p, w_proj, gain_f)


def _rows(start, dil):
    if dil == 1:
        return pl.ds(pl.multiple_of(start, A_BLOCK), A_BLOCK)
    return pl.ds(start, A_BLOCK, stride=dil)


def _dilated_attention_kernel(*refs):
    n_q = A_Q_PER_KV
    q_refs = refs[:n_q]
    kv_refs = refs[n_q:n_q + 12]
    o_ref = refs[n_q + 12]
    m_ref, l_ref, acc_ref, bias_ref = refs[n_q + 13:]
    blk = A_BLOCK
    rows = n_q * blk
    exp2_scale = HEAD_DIM ** -0.5 * math.log2(math.e)

    qi = lax.broadcasted_iota(jnp.int32, (rows, 2 * blk), 0) % blk
    kj = lax.broadcasted_iota(jnp.int32, (rows, 2 * blk), 1)
    band = (kj >= qi) & (kj <= qi + blk)
    bias_ref[0] = jnp.where(band, 0.0, -jnp.inf)
    bias_ref[1] = jnp.where(band & (kj >= blk), 0.0, -jnp.inf)
    first_bias = (pl.program_id(1) == 0).astype(jnp.int32)

    def scores(dil, start, kp_ref, kp_start, kc_ref, vp_ref, vc_ref, bias_idx):
        sel = _rows(start, dil)
        psel = _rows(kp_start, dil)
        q = jnp.concatenate([q_ref[sel, :] for q_ref in q_refs], axis=0).astype(BF16)
        k = jnp.concatenate([kp_ref[psel, :], kc_ref[sel, :]], axis=0).astype(BF16)
        v = jnp.concatenate([vp_ref[psel, :], vc_ref[sel, :]], axis=0).astype(BF16)
        s = lax.dot_general(q, k, (((1,), (1,)), ((), ())),
                            preferred_element_type=F32) + bias_ref[bias_idx]
        return s, v

    def softmax_pv(s, v):
        m_blk = jnp.max(s, axis=-1, keepdims=True)
        p = jnp.exp2((s - m_blk) * exp2_scale)
        l_blk = jnp.sum(p, axis=-1, keepdims=True)
        return m_blk, l_blk, jnp.dot(p.astype(BF16), v, preferred_element_type=F32)

    def merge(dil, mode, start, m_blk, l_blk, pv):
        sel = _rows(start, dil)
        if mode == "init":
            m_new = jnp.broadcast_to(m_blk, (rows, HEAD_DIM))
            l_new = jnp.broadcast_to(l_blk, (rows, HEAD_DIM))
            acc_new = pv
        else:
            m_old = jnp.concatenate([m_ref.at[h][sel, :] for h in range(n_q)], axis=0)
            l_old = jnp.concatenate([l_ref.at[h][sel, :] for h in range(n_q)], axis=0)
            acc_old = jnp.concatenate([acc_ref.at[h][sel, :] for h in range(n_q)], axis=0)
            m_new = jnp.maximum(m_old, m_blk)
            alpha = jnp.exp2((m_old - m_new) * exp2_scale)
            beta = jnp.exp2((m_blk - m_new) * exp2_scale)
            l_new = alpha * l_old + beta * l_blk
            acc_new = alpha * acc_old + beta * pv
        for h in range(n_q):
            r = slice(h * blk, (h + 1) * blk)
            if mode == "final":
                o_ref[sel, h * HEAD_DIM:(h + 1) * HEAD_DIM] = (
                    acc_new[r] / l_new[r]).astype(o_ref.dtype)
            else:
                m_ref.at[h][sel, :] = m_new[r]
                l_ref.at[h][sel, :] = l_new[r]
                acc_ref.at[h][sel, :] = acc_new[r]

    def attend(dil, mode, blocks):
        loaded = [scores(dil, *block) for block in blocks]
        stats = [softmax_pv(s, v) for s, v in loaded]
        for block, stat in zip(blocks, stats):
            merge(dil, mode, block[0], *stat)

    modes = {16: "init", 4: "update", 1: "final"}
    for branch, (_, dil) in enumerate(A_BRANCHES):
        kc_ref, kp_ref, vc_ref, vp_ref = kv_refs[4 * branch:4 * branch + 4]
        span = dil * blk
        n_later = dil * (A_TILE // span - 1)
        mode = modes[dil]

        def first_block(r, kc_ref=kc_ref, kp_ref=kp_ref, vc_ref=vc_ref, vp_ref=vp_ref):
            return (r, kp_ref, r, kc_ref, vp_ref, vc_ref, first_bias)

        def later_block(i, dil=dil, span=span, kc_ref=kc_ref, vc_ref=vc_ref):
            start = i % dil + span * (i // dil + 1)
            return (start, kc_ref, start - span, kc_ref, vc_ref, vc_ref, 0)

        def first_pair(i, carry, dil=dil, mode=mode, first_block=first_block):
            attend(dil, mode, [first_block(2 * i), first_block(2 * i + 1)])
            return carry

        def later_pair(i, carry, offset, dil=dil, mode=mode, later_block=later_block):
            attend(dil, mode, [later_block(2 * i + offset), later_block(2 * i + 1 + offset)])
            return carry

        if dil > 1:
            lax.fori_loop(0, dil // 2, first_pair, 0)
            if n_later:
                lax.fori_loop(0, n_later // 2, functools.partial(later_pair, offset=0), 0)
        else:
            attend(dil, mode, [first_block(0), later_block(0)])
            lax.fori_loop(0, n_later // 2, functools.partial(later_pair, offset=1), 0)


def _dilated_attention(qkv, b, s):
    c = qkv.shape[-1]
    n_kv = (c // HEAD_DIM) // (A_Q_PER_KV + 2 * len(A_BRANCHES))
    n_qh = n_kv * A_Q_PER_KV
    d = n_qh * HEAD_DIM
    qkv = qkv.reshape(b, s, c)
    tiles = s // A_TILE

    def q_spec(h):
        return pl.BlockSpec((None, A_TILE, HEAD_DIM),
                            lambda bi, t, g: (bi, t, g * A_Q_PER_KV + h))

    def kv_specs(dil, is_v):
        col = n_qh + (2 * A_BRANCH_COLUMN[dil] + is_v) * n_kv
        ratio = A_TILE // (dil * A_BLOCK)
        cur = pl.BlockSpec((None, A_TILE, HEAD_DIM), lambda bi, t, g: (bi, t, col + g))
        prev = pl.BlockSpec((None, dil * A_BLOCK, HEAD_DIM),
                            lambda bi, t, g: (bi, jnp.maximum(t * ratio - 1, 0), col + g))
        return [cur, prev]

    in_specs = [q_spec(h) for h in range(A_Q_PER_KV)]
    for _, dil in A_BRANCHES:
        in_specs += kv_specs(dil, 0) + kv_specs(dil, 1)
    state = pltpu.VMEM((A_Q_PER_KV, A_TILE, HEAD_DIM), F32)
    out = pl.pallas_call(
        _dilated_attention_kernel,
        grid=(b, tiles, n_kv),
        in_specs=in_specs,
        out_specs=pl.BlockSpec((None, A_TILE, A_Q_PER_KV * HEAD_DIM),
                               lambda bi, t, g: (bi, t, g)),
        out_shape=jax.ShapeDtypeStruct((b, s, d), BF16),
        scratch_shapes=[state, state, state,
                        pltpu.VMEM((2, A_Q_PER_KV * A_BLOCK, 2 * A_BLOCK), F32)],
        compiler_params=_params(("parallel", "parallel", "parallel"), 56),
        name="dilated_attention",
    )(*([qkv] * len(in_specs)))
    return out.reshape(b * s, d)


def _diff_attention_kernel(q_ref, k_ref, v_ref, lam_ref, g_ref, o_ref,
                           vt_ref, acc_ref, sa_ref, sb_ref, *, tq, lambda_init):
    qi = pl.program_id(2)
    n_chunks = v_ref.shape[1] // tq
    exp2_scale = HEAD_DIM ** -0.5 * math.log2(math.e)

    @pl.when(qi == 0)
    def _():
        def transpose_chunk(kc, carry):
            start = pl.multiple_of(kc * tq, tq)
            vt_ref[:, pl.ds(start, tq)] = v_ref[0, pl.ds(start, tq), :].T
            return carry
        lax.fori_loop(0, n_chunks, transpose_chunk, 0)

    acc_ref[...] = jnp.zeros(acc_ref.shape, F32)
    q = q_ref[0]

    def scores_into(s_ref, kc):
        start = pl.multiple_of(kc * tq, tq)
        k = k_ref[0, pl.ds(start, tq), :]
        for c in range(2):
            lanes = slice(c * HEAD_DIM, (c + 1) * HEAD_DIM)
            s_ref[c] = lax.dot_general(k[:, lanes], q[:, lanes], (((1,), (1,)), ((), ())),
                                       preferred_element_type=F32)

    def attend(s_ref, kc, state, masked):
        start = pl.multiple_of(kc * tq, tq)
        vt = vt_ref[:, pl.ds(start, tq)]
        new_state = []
        for c in range(2):
            m_old, l_old = state[c]
            st = s_ref[c]
            if masked:
                key = lax.broadcasted_iota(jnp.int32, st.shape, 0)
                query = lax.broadcasted_iota(jnp.int32, st.shape, 1)
                st = jnp.where(key <= query, st, -jnp.inf)
            m_new = jnp.maximum(m_old, jnp.max(st, axis=0, keepdims=True))
            alpha = jnp.exp2((m_old - m_new) * exp2_scale)
            pt = jnp.exp2((st - m_new) * exp2_scale)
            l_new = alpha * l_old + jnp.sum(pt, axis=0, keepdims=True)
            acc_ref[c] = alpha * acc_ref[c] + jnp.dot(vt, pt.astype(BF16),
                                                      preferred_element_type=F32)
            new_state.append((m_new, l_new))
        return tuple(new_state)

    def chunk_pair(j, state):
        scores_into(sb_ref, 2 * j + 1)
        state = attend(sa_ref, 2 * j, state, False)
        scores_into(sa_ref, 2 * j + 2)
        return attend(sb_ref, 2 * j + 1, state, False)

    def even_tail(state):
        return attend(sa_ref, qi, state, True)

    def odd_tail(state):
        scores_into(sb_ref, qi)
        state = attend(sa_ref, qi - 1, state, False)
        return attend(sb_ref, qi, state, True)

    init = ((jnp.full((1, tq), -jnp.inf, F32), jnp.zeros((1, tq), F32)),) * 2
    scores_into(sa_ref, 0)
    state = lax.fori_loop(0, qi // 2, chunk_pair, init)
    (_, l0), (_, l1) = lax.cond(qi % 2 == 0, even_tail, odd_tail, state)

    lp = lam_ref[...]
    lam = (jnp.exp(jnp.sum(lp[0:1] * lp[1:2], axis=-1, keepdims=True))
           - jnp.exp(jnp.sum(lp[2:3] * lp[3:4], axis=-1, keepdims=True)) + lambda_init)
    ot = acc_ref[0] * (1.0 / l0) - lam * (acc_ref[1] * (1.0 / l1))
    o = _rms_normalize(ot.T, g_ref[...], SUBLN_EPS) * (1.0 - lambda_init)
    o_ref[0] = o.astype(o_ref.dtype)


def _diff_attention(qkv, lam_params, subln_g, wlayer, lambda_init, b, s, tq):
    d = qkv.shape[-1] // 3
    hw = 2 * HEAD_DIM
    n_heads = d // hw
    qkv = qkv.reshape(b, s, 3 * d)
    out = pl.pallas_call(
        functools.partial(_diff_attention_kernel, tq=tq, lambda_init=lambda_init),
        grid=(b, n_heads, s // tq),
        in_specs=[
            pl.BlockSpec((1, tq, hw), lambda bi, h, i: (bi, i, h)),
            pl.BlockSpec((1, s, hw), lambda bi, h, i: (bi, 0, n_heads + h)),
            pl.BlockSpec((1, s, hw), lambda bi, h, i: (bi, 0, 2 * n_heads + h)),
            pl.BlockSpec((None, 4, HEAD_DIM), lambda bi, h, i: (wlayer, 0, 0)),
            pl.BlockSpec((None, 1, hw), lambda bi, h, i: (wlayer, 0, 0)),
        ],
        out_specs=pl.BlockSpec((1, tq, hw), lambda bi, h, i: (bi, i, h)),
        out_shape=jax.ShapeDtypeStruct((b, s, d), BF16),
        scratch_shapes=[pltpu.VMEM((hw, s), BF16), pltpu.VMEM((2, hw, tq), F32),
                        pltpu.VMEM((2, tq, tq), F32), pltpu.VMEM((2, tq, tq), F32)],
        compiler_params=_params(("parallel", "parallel", "arbitrary"), 40),
        name="diff_attention",
    )(qkv, qkv, qkv, lam_params, subln_g)
    return out.reshape(b * s, d)


def _conv_proj_kernel(bg_ref, cg_ref, u_ref, cgh_ref, uh_ref, cw_ref, w_ref, h_ref, o_ref,
                      z_ref, *, tiles_per_seq):
    tm = bg_ref.shape[0]
    halo = cgh_ref.shape[0]
    seq_start = pl.program_id(0) % tiles_per_seq == 0
    zh = cgh_ref[...] * uh_ref[...]
    z_ref[0:halo, :] = jnp.where(seq_start, 0.0, zh)
    z_ref[halo:, :] = cg_ref[...] * u_ref[...]
    cw = cw_ref[...]
    conv = (cw[0:1] * z_ref[pl.ds(halo - 2, tm), :]
            + cw[1:2] * z_ref[pl.ds(halo - 1, tm), :]
            + cw[2:3] * z_ref[pl.ds(halo, tm), :])
    y = (bg_ref[...] * conv).astype(BF16)
    o_ref[...] = h_ref[...] + jnp.dot(y, w_ref[...], preferred_element_type=F32)


def _conv_proj(proj, conv_w, w_out, wlayer, h, s, tm):
    m, d = h.shape
    halo = 8
    ratio = tm // halo
    return pl.pallas_call(
        functools.partial(_conv_proj_kernel, tiles_per_seq=s // tm),
        grid=(m // tm,),
        in_specs=[
            pl.BlockSpec((tm, d), lambda i: (i, 0)),
            pl.BlockSpec((tm, d), lambda i: (i, 1)),
            pl.BlockSpec((tm, d), lambda i: (i, 2)),
            pl.BlockSpec((halo, d), lambda i: (jnp.maximum(i * ratio - 1, 0), 1)),
            pl.BlockSpec((halo, d), lambda i: (jnp.maximum(i * ratio - 1, 0), 2)),
            pl.BlockSpec((None, 3, d), lambda i: (wlayer, 0, 0)),
            pl.BlockSpec((None, d, d), lambda i: (wlayer, 0, 0)),
            pl.BlockSpec((tm, d), lambda i: (i, 0)),
        ],
        out_specs=pl.BlockSpec((tm, d), lambda i: (i, 0)),
        out_shape=jax.ShapeDtypeStruct((m, d), F32),
        scratch_shapes=[pltpu.VMEM((tm + halo, d), F32)],
        compiler_params=_params(("parallel",), 56),
        name="conv_proj",
    )(proj, proj, proj, proj, proj, conv_w, w_out, h)


def kernel(x, p, norm_ffn1, w_ffn1_in, w_ffn1_out, norm_mix, a_w_qkv, a_w_o, b_w_qkv, b_w_o,
           b_lambda, b_subln, c_w_in, c_conv_w, c_w_out, norm_ffn2, w_ffn2_in, w_ffn2_out,
           norm_ple, w_ple_gate, b_ple_gate, w_ple_proj, norm_f):
    b, s, d = x.shape
    depth = p.shape[0]
    m = b * s
    h = x.reshape(m, d)
    p = p.reshape(depth, m, p.shape[-1])

    def row(t):
        return t.reshape(t.shape[0], 1, t.shape[-1])

    norm_ffn1, norm_mix, norm_ffn2, norm_ple, b_ple_gate, b_subln = (
        row(t) for t in (norm_ffn1, norm_mix, norm_ffn2, norm_ple, b_ple_gate, b_subln))
    norm_f = norm_f.reshape(1, d)
    (w_ffn1_in, w_ffn1_out, a_w_qkv, a_w_o, b_w_qkv, b_w_o, c_w_in, c_w_out,
     w_ffn2_in, w_ffn2_out, w_ple_gate, w_ple_proj) = (
        t.astype(BF16) for t in (w_ffn1_in, w_ffn1_out, a_w_qkv, a_w_o, b_w_qkv, b_w_o,
                                 c_w_in, c_w_out, w_ffn2_in, w_ffn2_out, w_ple_gate,
                                 w_ple_proj))

    for i in range(depth):
        h = _ffn(h, norm_ffn1, w_ffn1_in, w_ffn1_out, i, 1024, 512)
        kind, j = i % N_MIXERS, i // N_MIXERS
        if kind == 0:
            qkv = _norm_matmul(h, norm_mix, a_w_qkv, i, j, F32, 256, 1024)
            mix = _dilated_attention(qkv, b, s)
            h = _proj_residual(mix, a_w_o, j, h, 512, 1024)
        elif kind == 1:
            lambda_init = 0.8 - 0.6 * math.exp(-0.3 * i)
            qkv = _norm_matmul(h, norm_mix, b_w_qkv, i, j, BF16, 512, 1024)
            mix = _diff_attention(qkv, b_lambda, b_subln, j, lambda_init, b, s, 512)
            h = _proj_residual(mix, b_w_o, j, h, 512, 1024)
        else:
            proj = _norm_matmul(h, norm_mix, c_w_in, i, j, F32, 256, 1024)
            h = _conv_proj(proj, c_conv_w, c_w_out, j, h, s, 256)
        h = _ffn(h, norm_ffn2, w_ffn2_in, w_ffn2_out, i, 1024, 512)
        h = _ple(h, norm_ple, w_ple_gate, b_ple_gate, p, w_ple_proj, norm_f, i,
                 i == depth - 1, 512)
    return h.reshape(b, s, d)
```

```python
import functools
import math

import jax
import jax.numpy as jnp
from jax import lax
from jax.experimental import pallas as pl
from jax.experimental.pallas import tpu as pltpu

BF16 = jnp.bfloat16
F32 = jnp.float32

HEAD_DIM = 128
A_Q_PER_KV = 4
A_BRANCHES = ((2048, 16), (512, 4), (128, 1))
A_BRANCH_COLUMN = {1: 0, 4: 1, 16: 2}
A_BLOCK = 128
A_TILE = 2048
RMS_EPS = 1e-6
SUBLN_EPS = 1e-5
N_MIXERS = 3

V7X_VMEM_BYTES = 64 * 1024 * 1024
MIB = 1024 * 1024


def _params(semantics, vmem_mib):
    return pltpu.CompilerParams(dimension_semantics=semantics,
                                vmem_limit_bytes=vmem_mib * MIB)


def _rms_normalize(x, gain, eps):
    ms = jnp.mean(x * x, axis=-1, keepdims=True)
    return x * lax.rsqrt(ms + eps) * gain


def _sigmoid(x):
    return 1.0 / (1.0 + jnp.exp(-x))


def _ffn_kernel(*refs, cast_next):
    if cast_next:
        (h_ref, g_ref, wg_ref, wu_ref, wo_ref, next_in_ref, next_out_ref,
         o_ref, next_in_bf_ref, next_out_bf_ref, rs_ref) = refs
        next_in_bf_ref[...] = next_in_ref[...].astype(BF16)
        next_out_bf_ref[...] = next_out_ref[...].astype(BF16)
    else:
        h_ref, g_ref, wg_ref, wu_ref, wo_ref, o_ref, rs_ref = refs
    first = pl.program_id(1) == 0

    @pl.when(first)
    def _():
        h = h_ref[...]
        rs_ref[...] = lax.rsqrt(jnp.mean(h * h, axis=-1, keepdims=True) + RMS_EPS)

    h = h_ref[...]
    hn = (h * rs_ref[...] * g_ref[...]).astype(BF16)
    gate = jnp.dot(hn, wg_ref[...], preferred_element_type=F32)
    up = jnp.dot(hn, wu_ref[...], preferred_element_type=F32)
    act = (0.5 * gate * _sigmoid(gate)) * up
    base = jnp.where(first, h, o_ref[...])
    o_ref[...] = base + jnp.dot(act.astype(BF16), wo_ref[...], preferred_element_type=F32)


def _ffn(h, gain, layer, w_in, w_out, wlayer, next_w, tm, tf):
    m, d = h.shape
    f = w_out.shape[1]
    nf = f // tf
    n_i = m // tm
    in_specs = [
        pl.BlockSpec((tm, d), lambda i, j: (i, 0)),
        pl.BlockSpec((None, 1, d), lambda i, j: (layer, 0, 0)),
        pl.BlockSpec((None, d, tf), lambda i, j: (wlayer, 0, j)),
        pl.BlockSpec((None, d, tf), lambda i, j: (wlayer, 0, nf + j)),
        pl.BlockSpec((None, tf, d), lambda i, j: (wlayer, j, 0)),
    ]
    out_specs = [pl.BlockSpec((tm, d), lambda i, j: (i, 0))]
    out_shape = [jax.ShapeDtypeStruct((m, d), F32)]
    args = [h, gain, w_in, w_in, w_out]
    if next_w is not None:
        next_in, next_out, nl = next_w
        in_specs += [pl.BlockSpec((None, d // n_i, 2 * tf), lambda i, j: (nl, i, j)),
                     pl.BlockSpec((None, tf, d // n_i), lambda i, j: (nl, j, i))]
        out_specs += [pl.BlockSpec((None, d // n_i, 2 * tf), lambda i, j: (0, i, j)),
                      pl.BlockSpec((None, tf, d // n_i), lambda i, j: (0, j, i))]
        out_shape += [jax.ShapeDtypeStruct((1, d, 2 * f), BF16),
                      jax.ShapeDtypeStruct((1, f, d), BF16)]
        args += [next_in, next_out]
    return pl.pallas_call(
        functools.partial(_ffn_kernel, cast_next=next_w is not None),
        grid=(n_i, nf),
        in_specs=in_specs,
        out_specs=out_specs,
        out_shape=out_shape,
        scratch_shapes=[pltpu.VMEM((tm, 1), F32)],
        compiler_params=_params(("parallel", "arbitrary"), 56),
        name="ffn",
    )(*args)


def _norm_matmul_kernel(h_ref, g_ref, w_ref, o_ref, *, tn):
    hn = _rms_normalize(h_ref[...], g_ref[...], RMS_EPS).astype(BF16)
    for n0 in range(0, o_ref.shape[1], tn):
        o_ref[:, n0:n0 + tn] = jnp.dot(hn, w_ref[:, n0:n0 + tn],
                                       preferred_element_type=F32).astype(o_ref.dtype)


def _norm_matmul(h, gain, w, layer, wlayer, out_dtype, tm, tn):
    m, d = h.shape
    n = w.shape[2]
    return pl.pallas_call(
        functools.partial(_norm_matmul_kernel, tn=tn),
        grid=(m // tm,),
        in_specs=[
            pl.BlockSpec((tm, d), lambda i: (i, 0)),
            pl.BlockSpec((None, 1, d), lambda i: (layer, 0, 0)),
            pl.BlockSpec((None, d, n), lambda i: (wlayer, 0, 0),
                         pipeline_mode=pl.Buffered(1)),
        ],
        out_specs=pl.BlockSpec((tm, n), lambda i: (i, 0)),
        out_shape=jax.ShapeDtypeStruct((m, n), out_dtype),
        compiler_params=_params(("parallel",), 56),
        name="norm_matmul",
    )(h, gain, w)


def _proj_residual_kernel(x_ref, w_ref, h_ref, o_ref, *, tn):
    x = x_ref[...]
    for n0 in range(0, o_ref.shape[1], tn):
        cols = slice(n0, n0 + tn)
        o_ref[:, cols] = h_ref[:, cols] + jnp.dot(x, w_ref[:, cols],
                                                  preferred_element_type=F32)


def _proj_residual(x, w, wlayer, h, tm, tn):
    m, k = x.shape
    n = w.shape[2]
    return pl.pallas_call(
        functools.partial(_proj_residual_kernel, tn=tn),
        grid=(m // tm,),
        in_specs=[
            pl.BlockSpec((tm, k), lambda i: (i, 0)),
            pl.BlockSpec((None, k, n), lambda i: (wlayer, 0, 0),
                         pipeline_mode=pl.Buffered(1)),
            pl.BlockSpec((tm, n), lambda i: (i, 0)),
        ],
        out_specs=pl.BlockSpec((tm, n), lambda i: (i, 0)),
        out_shape=jax.ShapeDtypeStruct((m, n), F32),
        compiler_params=_params(("parallel",), 48),
        name="proj_residual",
    )(x, w, h)


def _ple_kernel(h_ref, g_ref, wg_ref, b_ref, p_ref, wp_ref, gf_ref, o_ref, *, final_norm):
    h = h_ref[...]
    hn = _rms_normalize(h, g_ref[...], RMS_EPS).astype(BF16)
    gate = _sigmoid(jnp.dot(hn, wg_ref[...], preferred_element_type=F32) + b_ref[...])
    proj = jnp.dot(p_ref[...].astype(BF16), wp_ref[...], preferred_element_type=F32)
    out = h + gate * proj
    if final_norm:
        out = _rms_normalize(out, gf_ref[...], RMS_EPS)
    o_ref[...] = out


def _ple(h, gain, w_gate, b_gate, p, w_proj, gain_f, layer, final_norm, tm):
    m, d = h.shape
    e = p.shape[2]
    nblk = m // tm
    return pl.pallas_call(
        functools.partial(_ple_kernel, final_norm=final_norm),
        grid=(nblk,),
        in_specs=[
            pl.BlockSpec((tm, d), lambda i: (i, 0)),
            pl.BlockSpec((None, 1, d), lambda i: (layer, 0, 0)),
            pl.BlockSpec((None, d, d), lambda i: (layer, 0, 0)),
            pl.BlockSpec((None, 1, d), lambda i: (layer, 0, 0)),
            pl.BlockSpec((None, tm, e), lambda i: (layer, i, 0)),
            pl.BlockSpec((None, e, d), lambda i: (layer, 0, 0)),
            pl.BlockSpec((1, d), lambda i: (0, 0)),
        ],
        out_specs=pl.BlockSpec((tm, d), lambda i: (i, 0)),
        out_shape=jax.ShapeDtypeStruct((m, d), F32),
        compiler_params=_params(("parallel",), 56),
        name="ple",
    )(h, gain, w_gate, b_gate, p, w_proj, gain_f)


def _rows(start, dil):
    if dil == 1:
        return pl.ds(pl.multiple_of(start, A_BLOCK), A_BLOCK)
    return pl.ds(start, A_BLOCK, stride=dil)


def _dilated_attention_kernel(*refs):
    n_q = A_Q_PER_KV
    q_refs = refs[:n_q]
    kv_refs = refs[n_q:n_q + 12]
    o_ref = refs[n_q + 12]
    m_ref, l_ref, acc_ref, bias_ref = refs[n_q + 13:]
    blk = A_BLOCK
    rows = n_q * blk
    exp2_scale = HEAD_DIM ** -0.5 * math.log2(math.e)

    qi = lax.broadcasted_iota(jnp.int32, (rows, 2 * blk), 0) % blk
    kj = lax.broadcasted_iota(jnp.int32, (rows, 2 * blk), 1)
    band = (kj >= qi) & (kj <= qi + blk)
    bias_ref[0] = jnp.where(band, 0.0, -jnp.inf)
    bias_ref[1] = jnp.where(band & (kj >= blk), 0.0, -jnp.inf)
    first_bias = (pl.program_id(1) == 0).astype(jnp.int32)

    def scores(dil, start, kp_ref, kp_start, kc_ref, vp_ref, vc_ref, bias_idx):
        sel = _rows(start, dil)
        psel = _rows(kp_start, dil)
        q = jnp.concatenate([q_ref[sel, :] for q_ref in q_refs], axis=0).astype(BF16)
        k = jnp.concatenate([kp_ref[psel, :], kc_ref[sel, :]], axis=0).astype(BF16)
        v = jnp.concatenate([vp_ref[psel, :], vc_ref[sel, :]], axis=0).astype(BF16)
        s = lax.dot_general(q, k, (((1,), (1,)), ((), ())),
                            preferred_element_type=F32) + bias_ref[bias_idx]
        return s, v

    def softmax_pv(s, v):
        m_blk = jnp.max(s, axis=-1, keepdims=True)
        p = jnp.exp2((s - m_blk) * exp2_scale)
        l_blk = jnp.sum(p, axis=-1, keepdims=True)
        return m_blk, l_blk, jnp.dot(p.astype(BF16), v, preferred_element_type=F32)

    def merge(dil, mode, start, m_blk, l_blk, pv):
        sel = _rows(start, dil)
        if mode == "init":
            m_new = jnp.broadcast_to(m_blk, (rows, HEAD_DIM))
            l_new = jnp.broadcast_to(l_blk, (rows, HEAD_DIM))
            acc_new = pv
        else:
            m_old = jnp.concatenate([m_ref.at[h][sel, :] for h in range(n_q)], axis=0)
            l_old = jnp.concatenate([l_ref.at[h][sel, :] for h in range(n_q)], axis=0)
            acc_old = jnp.concatenate([acc_ref.at[h][sel, :] for h in range(n_q)], axis=0)
            m_new = jnp.maximum(m_old, m_blk)
            alpha = jnp.exp2((m_old - m_new) * exp2_scale)
            beta = jnp.exp2((m_blk - m_new) * exp2_scale)
            l_new = alpha * l_old + beta * l_blk
            acc_new = alpha * acc_old + beta * pv
        for h in range(n_q):
            r = slice(h * blk, (h + 1) * blk)
            if mode == "final":
                o_ref[sel, h * HEAD_DIM:(h + 1) * HEAD_DIM] = (
                    acc_new[r] / l_new[r]).astype(o_ref.dtype)
            else:
                m_ref.at[h][sel, :] = m_new[r]
                l_ref.at[h][sel, :] = l_new[r]
                acc_ref.at[h][sel, :] = acc_new[r]

    def attend(dil, mode, blocks):
        loaded = [scores(dil, *block) for block in blocks]
        stats = [softmax_pv(s, v) for s, v in loaded]
        for block, stat in zip(blocks, stats):
            merge(dil, mode, block[0], *stat)

    modes = {16: "init", 4: "update", 1: "final"}
    for branch, (_, dil) in enumerate(A_BRANCHES):
        kc_ref, kp_ref, vc_ref, vp_ref = kv_refs[4 * branch:4 * branch + 4]
        span = dil * blk
        n_later = dil * (A_TILE // span - 1)
        mode = modes[dil]

        def first_block(r, kc_ref=kc_ref, kp_ref=kp_ref, vc_ref=vc_ref, vp_ref=vp_ref):
            return (r, kp_ref, r, kc_ref, vp_ref, vc_ref, first_bias)

        def later_block(i, dil=dil, span=span, kc_ref=kc_ref, vc_ref=vc_ref):
            start = i % dil + span * (i // dil + 1)
            return (start, kc_ref, start - span, kc_ref, vc_ref, vc_ref, 0)

        def first_pair(i, carry, dil=dil, mode=mode, first_block=first_block):
            attend(dil, mode, [first_block(2 * i), first_block(2 * i + 1)])
            return carry

        def later_pair(i, carry, offset, dil=dil, mode=mode, later_block=later_block):
            attend(dil, mode, [later_block(2 * i + offset), later_block(2 * i + 1 + offset)])
            return carry

        if dil > 1:
            lax.fori_loop(0, dil // 2, first_pair, 0)
            if n_later:
                lax.fori_loop(0, n_later // 2, functools.partial(later_pair, offset=0), 0)
        else:
            attend(dil, mode, [first_block(0), later_block(0)])
            lax.fori_loop(0, n_later // 2, functools.partial(later_pair, offset=1), 0)


def _dilated_attention(qkv, b, s):
    c = qkv.shape[-1]
    n_kv = (c // HEAD_DIM) // (A_Q_PER_KV + 2 * len(A_BRANCHES))
    n_qh = n_kv * A_Q_PER_KV
    d = n_qh * HEAD_DIM
    qkv = qkv.reshape(b, s, c)
    tiles = s // A_TILE

    def q_spec(h):
        return pl.BlockSpec((None, A_TILE, HEAD_DIM),
                            lambda bi, t, g: (bi, t, g * A_Q_PER_KV + h))

    def kv_specs(dil, is_v):
        col = n_qh + (2 * A_BRANCH_COLUMN[dil] + is_v) * n_kv
        ratio = A_TILE // (dil * A_BLOCK)
        cur = pl.BlockSpec((None, A_TILE, HEAD_DIM), lambda bi, t, g: (bi, t, col + g))
        prev = pl.BlockSpec((None, dil * A_BLOCK, HEAD_DIM),
                            lambda bi, t, g: (bi, jnp.maximum(t * ratio - 1, 0), col + g))
        return [cur, prev]

    in_specs = [q_spec(h) for h in range(A_Q_PER_KV)]
    for _, dil in A_BRANCHES:
        in_specs += kv_specs(dil, 0) + kv_specs(dil, 1)
    state = pltpu.VMEM((A_Q_PER_KV, A_TILE, HEAD_DIM), F32)
    out = pl.pallas_call(
        _dilated_attention_kernel,
        grid=(b, tiles, n_kv),
        in_specs=in_specs,
        out_specs=pl.BlockSpec((None, A_TILE, A_Q_PER_KV * HEAD_DIM),
                               lambda bi, t, g: (bi, t, g)),
        out_shape=jax.ShapeDtypeStruct((b, s, d), BF16),
        scratch_shapes=[state, state, state,
                        pltpu.VMEM((2, A_Q_PER_KV * A_BLOCK, 2 * A_BLOCK), F32)],
        compiler_params=_params(("parallel", "parallel", "parallel"), 56),
        name="dilated_attention",
    )(*([qkv] * len(in_specs)))
    return out.reshape(b * s, d)


def _diff_attention_kernel(q_ref, k_ref, v_ref, lam_ref, g_ref, o_ref,
                           vt_ref, acc_ref, sa_ref, sb_ref, *, tq, lambda_init):
    qi = pl.program_id(2)
    n_chunks = v_ref.shape[1] // tq
    exp2_scale = HEAD_DIM ** -0.5 * math.log2(math.e)

    @pl.when(qi == 0)
    def _():
        def transpose_chunk(kc, carry):
            start = pl.multiple_of(kc * tq, tq)
            vt_ref[:, pl.ds(start, tq)] = v_ref[0, pl.ds(start, tq), :].T
            return carry
        lax.fori_loop(0, n_chunks, transpose_chunk, 0)

    acc_ref[...] = jnp.zeros(acc_ref.shape, F32)
    q = q_ref[0]

    def scores_into(s_ref, kc):
        start = pl.multiple_of(kc * tq, tq)
        k = k_ref[0, pl.ds(start, tq), :]
        for c in range(2):
            lanes = slice(c * HEAD_DIM, (c + 1) * HEAD_DIM)
            s_ref[c] = lax.dot_general(k[:, lanes], q[:, lanes], (((1,), (1,)), ((), ())),
                                       preferred_element_type=F32)

    def attend(s_ref, kc, state, masked):
        start = pl.multiple_of(kc * tq, tq)
        vt = vt_ref[:, pl.ds(start, tq)]
        new_state = []
        for c in range(2):
            m_old, l_old = state[c]
            st = s_ref[c]
            if masked:
                key = lax.broadcasted_iota(jnp.int32, st.shape, 0)
                query = lax.broadcasted_iota(jnp.int32, st.shape, 1)
                st = jnp.where(key <= query, st, -jnp.inf)
            m_new = jnp.maximum(m_old, jnp.max(st, axis=0, keepdims=True))
            alpha = jnp.exp2((m_old - m_new) * exp2_scale)
            pt = jnp.exp2((st - m_new) * exp2_scale)
            l_new = alpha * l_old + jnp.sum(pt, axis=0, keepdims=True)
            acc_ref[c] = alpha * acc_ref[c] + jnp.dot(vt, pt.astype(BF16),
                                                      preferred_element_type=F32)
            new_state.append((m_new, l_new))
        return tuple(new_state)

    def chunk_pair(j, state):
        scores_into(sb_ref, 2 * j + 1)
        state = attend(sa_ref, 2 * j, state, False)
        scores_into(sa_ref, 2 * j + 2)
        return attend(sb_ref, 2 * j + 1, state, False)

    def even_tail(state):
        return attend(sa_ref, qi, state, True)

    def odd_tail(state):
        scores_into(sb_ref, qi)
        state = attend(sa_ref, qi - 1, state, False)
        return attend(sb_ref, qi, state, True)

    init = ((jnp.full((1, tq), -jnp.inf, F32), jnp.zeros((1, tq), F32)),) * 2
    scores_into(sa_ref, 0)
    state = lax.fori_loop(0, qi // 2, chunk_pair, init)
    (_, l0), (_, l1) = lax.cond(qi % 2 == 0, even_tail, odd_tail, state)

    lp = lam_ref[...]
    lam = (jnp.exp(jnp.sum(lp[0:1] * lp[1:2], axis=-1, keepdims=True))
           - jnp.exp(jnp.sum(lp[2:3] * lp[3:4], axis=-1, keepdims=True)) + lambda_init)
    ot = acc_ref[0] * (1.0 / l0) - lam * (acc_ref[1] * (1.0 / l1))
    o = _rms_normalize(ot.T, g_ref[...], SUBLN_EPS) * (1.0 - lambda_init)
    o_ref[0] = o.astype(o_ref.dtype)


def _diff_attention(qkv, lam_params, subln_g, wlayer, lambda_init, b, s, tq):
    d = qkv.shape[-1] // 3
    hw = 2 * HEAD_DIM
    n_heads = d // hw
    qkv = qkv.reshape(b, s, 3 * d)
    out = pl.pallas_call(
        functools.partial(_diff_attention_kernel, tq=tq, lambda_init=lambda_init),
        grid=(b, n_heads, s // tq),
        in_specs=[
            pl.BlockSpec((1, tq, hw), lambda bi, h, i: (bi, i, h)),
            pl.BlockSpec((1, s, hw), lambda bi, h, i: (bi, 0, n_heads + h)),
            pl.BlockSpec((1, s, hw), lambda bi, h, i: (bi, 0, 2 * n_heads + h)),
            pl.BlockSpec((None, 4, HEAD_DIM), lambda bi, h, i: (wlayer, 0, 0)),
            pl.BlockSpec((None, 1, hw), lambda bi, h, i: (wlayer, 0, 0)),
        ],
        out_specs=pl.BlockSpec((1, tq, hw), lambda bi, h, i: (bi, i, h)),
        out_shape=jax.ShapeDtypeStruct((b, s, d), BF16),
        scratch_shapes=[pltpu.VMEM((hw, s), BF16), pltpu.VMEM((2, hw, tq), F32),
                        pltpu.VMEM((2, tq, tq), F32), pltpu.VMEM((2, tq, tq), F32)],
        compiler_params=_params(("parallel", "parallel", "arbitrary"), 40),
        name="diff_attention",
    )(qkv, qkv, qkv, lam_params, subln_g)
    return out.reshape(b * s, d)


def _conv_proj_kernel(bg_ref, cg_ref, u_ref, cgh_ref, uh_ref, cw_ref, w_ref, h_ref, o_ref,
                      z_ref, *, tiles_per_seq):
    tm = bg_ref.shape[0]
    halo = cgh_ref.shape[0]
    seq_start = pl.program_id(0) % tiles_per_seq == 0
    zh = cgh_ref[...] * uh_ref[...]
    z_ref[0:halo, :] = jnp.where(seq_start, 0.0, zh)
    z_ref[halo:, :] = cg_ref[...] * u_ref[...]
    cw = cw_ref[...]
    conv = (cw[0:1] * z_ref[pl.ds(halo - 2, tm), :]
            + cw[1:2] * z_ref[pl.ds(halo - 1, tm), :]
            + cw[2:3] * z_ref[pl.ds(halo, tm), :])
    y = (bg_ref[...] * conv).astype(BF16)
    o_ref[...] = h_ref[...] + jnp.dot(y, w_ref[...], preferred_element_type=F32)


def _conv_proj(proj, conv_w, w_out, wlayer, h, s, tm):
    m, d = h.shape
    halo = 8
    ratio = tm // halo
    return pl.pallas_call(
        functools.partial(_conv_proj_kernel, tiles_per_seq=s // tm),
        grid=(m // tm,),
        in_specs=[
            pl.BlockSpec((tm, d), lambda i: (i, 0)),
            pl.BlockSpec((tm, d), lambda i: (i, 1)),
            pl.BlockSpec((tm, d), lambda i: (i, 2)),
            pl.BlockSpec((halo, d), lambda i: (jnp.maximum(i * ratio - 1, 0), 1)),
            pl.BlockSpec((halo, d), lambda i: (jnp.maximum(i * ratio - 1, 0), 2)),
            pl.BlockSpec((None, 3, d), lambda i: (wlayer, 0, 0)),
            pl.BlockSpec((None, d, d), lambda i: (wlayer, 0, 0)),
            pl.BlockSpec((tm, d), lambda i: (i, 0)),
        ],
        out_specs=pl.BlockSpec((tm, d), lambda i: (i, 0)),
        out_shape=jax.ShapeDtypeStruct((m, d), F32),
        scratch_shapes=[pltpu.VMEM((tm + halo, d), F32)],
        compiler_params=_params(("parallel",), 56),
        name="conv_proj",
    )(proj, proj, proj, proj, proj, conv_w, w_out, h)


def kernel(x, p, norm_ffn1, w_ffn1_in, w_ffn1_out, norm_mix, a_w_qkv, a_w_o, b_w_qkv, b_w_o,
           b_lambda, b_subln, c_w_in, c_conv_w, c_w_out, norm_ffn2, w_ffn2_in, w_ffn2_out,
           norm_ple, w_ple_gate, b_ple_gate, w_ple_proj, norm_f):
    b, s, d = x.shape
    depth = p.shape[0]
    m = b * s
    h = x.reshape(m, d)
    p = p.reshape(depth, m, p.shape[-1])

    def row(t):
        return t.reshape(t.shape[0], 1, t.shape[-1])

    norm_ffn1, norm_mix, norm_ffn2, norm_ple, b_ple_gate, b_subln = (
        row(t) for t in (norm_ffn1, norm_mix, norm_ffn2, norm_ple, b_ple_gate, b_subln))
    norm_f = norm_f.reshape(1, d)
    a_w_qkv, a_w_o, b_w_qkv, b_w_o, c_w_in, c_w_out, w_ple_gate, w_ple_proj = (
        t.astype(BF16) for t in (a_w_qkv, a_w_o, b_w_qkv, b_w_o, c_w_in, c_w_out,
                                 w_ple_gate, w_ple_proj))
    w_in, w_out = w_ffn1_in[:1].astype(BF16), w_ffn1_out[:1].astype(BF16)

    for i in range(depth):
        h, w_in, w_out = _ffn(h, norm_ffn1, i, w_in, w_out, 0,
                              (w_ffn2_in, w_ffn2_out, i), 1024, 512)
        kind, j = i % N_MIXERS, i // N_MIXERS
        if kind == 0:
            qkv = _norm_matmul(h, norm_mix, a_w_qkv, i, j, F32, 256, 1024)
            mix = _dilated_attention(qkv, b, s)
            h = _proj_residual(mix, a_w_o, j, h, 512, 1024)
        elif kind == 1:
            lambda_init = 0.8 - 0.6 * math.exp(-0.3 * i)
            qkv = _norm_matmul(h, norm_mix, b_w_qkv, i, j, BF16, 512, 1024)
            mix = _diff_attention(qkv, b_lambda, b_subln, j, lambda_init, b, s, 512)
            h = _proj_residual(mix, b_w_o, j, h, 512, 1024)
        else:
            proj = _norm_matmul(h, norm_mix, c_w_in, i, j, F32, 256, 1024)
            h = _conv_proj(proj, c_conv_w, c_w_out, j, h, s, 256)
        if i + 1 < depth:
            h, w_in, w_out = _ffn(h, norm_ffn2, i, w_in, w_out, 0,
                                  (w_ffn1_in, w_ffn1_out, i + 1), 1024, 512)
        else:
            h, = _ffn(h, norm_ffn2, i, w_in, w_out, 0, None, 1024, 512)
        h = _ple(h, norm_ple, w_ple_gate, b_ple_gate, p, w_ple_proj, norm_f, i,
                 i == depth - 1, 512)
    return h.reshape(b, s, d)
```

```python
import functools
import math
from typing import NamedTuple

import jax
import jax.numpy as jnp
from jax import lax
from jax.experimental import pallas as pl
from jax.experimental.pallas import tpu as pltpu

BF16 = jnp.bfloat16
F32 = jnp.float32

HEAD_DIM = 128
A_Q_PER_KV = 4
A_BRANCHES = ((2048, 16), (512, 4), (128, 1))
A_BRANCH_COLUMN = {1: 0, 4: 1, 16: 2}
A_BLOCK = 128
A_TILE = 2048
A_BLOCKS_PER_TRIP = 2
RMS_EPS = 1e-6
SUBLN_EPS = 1e-5
N_MIXERS = 3

MIB = 1024 * 1024
V7X_VMEM_MIB = 64
F32_SUBLANES = 8


class _Plan(NamedTuple):
    tm: int
    tn: int
    vmem_mib: int


_PLANS = {
    "ffn": _Plan(1024, 512, 60),
    "proj_a": _Plan(512, 1024, 60),
    "proj_b": _Plan(512, 1024, 56),
    "proj_c": _Plan(512, 1024, 60),
    "out_proj": _Plan(512, 1024, 48),
    "conv_proj": _Plan(512, 0, 56),
    "ple": _Plan(512, 0, 56),
    "diff_attention": _Plan(512, 0, 40),
    "dilated_attention": _Plan(A_TILE, 0, 56),
    "cast": _Plan(256, 0, 32),
    "cast_ffn_in": _Plan(128, 0, 32),
    "cast_ffn_out": _Plan(512, 0, 32),
}
assert all(plan.vmem_mib < V7X_VMEM_MIB for plan in _PLANS.values())


def _params(semantics, plan):
    return pltpu.CompilerParams(dimension_semantics=semantics,
                                vmem_limit_bytes=_PLANS[plan].vmem_mib * MIB)


def _rms_normalize(x, gain, eps):
    ms = jnp.mean(x * x, axis=-1, keepdims=True)
    return x * lax.rsqrt(ms + eps) * gain


def _sigmoid(x):
    return 1.0 / (1.0 + jnp.exp(-x))


def _ffn_kernel(*refs, cast_next):
    if cast_next:
        (h_ref, g_ref, wg_ref, wu_ref, wo_ref, next_in_ref, next_out_ref,
         o_ref, next_in_bf_ref, next_out_bf_ref, hn_ref) = refs
        next_in_bf_ref[...] = next_in_ref[...].astype(BF16)
        next_out_bf_ref[...] = next_out_ref[...].astype(BF16)
    else:
        h_ref, g_ref, wg_ref, wu_ref, wo_ref, o_ref, hn_ref = refs

    @pl.when(pl.program_id(1) == 0)
    def _():
        h = h_ref[...]
        hn_ref[...] = _rms_normalize(h, g_ref[...], RMS_EPS).astype(BF16)
        o_ref[...] = h

    hn = hn_ref[...]
    gate = jnp.dot(hn, wg_ref[...], preferred_element_type=F32)
    up = jnp.dot(hn, wu_ref[...], preferred_element_type=F32)
    act = (0.5 * gate * _sigmoid(gate)) * up
    o_ref[...] += jnp.dot(act.astype(BF16), wo_ref[...], preferred_element_type=F32)


def _ffn(h, gain, layer, w_in, w_out, wlayer, next_w):
    tm, tf, _ = _PLANS["ffn"]
    m, d = h.shape
    f = w_out.shape[1]
    nf = f // tf
    n_i = m // tm
    in_specs = [
        pl.BlockSpec((tm, d), lambda i, j: (i, 0)),
        pl.BlockSpec((None, 1, d), lambda i, j: (layer, 0, 0)),
        pl.BlockSpec((None, d, tf), lambda i, j: (wlayer, 0, j)),
        pl.BlockSpec((None, d, tf), lambda i, j: (wlayer, 0, nf + j)),
        pl.BlockSpec((None, tf, d), lambda i, j: (wlayer, j, 0)),
    ]
    out_specs = [pl.BlockSpec((tm, d), lambda i, j: (i, 0))]
    out_shape = [jax.ShapeDtypeStruct((m, d), F32)]
    args = [h, gain, w_in, w_in, w_out]
    if next_w is not None:
        next_in, next_out, nl = next_w
        out_rows = f // (n_i * nf)
        in_specs += [pl.BlockSpec((None, d // n_i, 2 * tf), lambda i, j: (nl, i, j)),
                     pl.BlockSpec((None, out_rows, d), lambda i, j: (nl, i * nf + j, 0))]
        out_specs += [pl.BlockSpec((None, d // n_i, 2 * tf), lambda i, j: (0, i, j)),
                      pl.BlockSpec((None, out_rows, d), lambda i, j: (0, i * nf + j, 0))]
        out_shape += [jax.ShapeDtypeStruct((1, d, 2 * f), BF16),
                      jax.ShapeDtypeStruct((1, f, d), BF16)]
        args += [next_in, next_out]
    return pl.pallas_call(
        functools.partial(_ffn_kernel, cast_next=next_w is not None),
        grid=(n_i, nf),
        in_specs=in_specs,
        out_specs=out_specs,
        out_shape=out_shape,
        scratch_shapes=[pltpu.VMEM((tm, d), BF16)],
        compiler_params=_params(("parallel", "arbitrary"), "ffn"),
        name="ffn",
    )(*args)


def _norm_matmul_kernel(h_ref, g_ref, w_ref, o_ref, *, tn, gated):
    hn = _rms_normalize(h_ref[...], g_ref[...], RMS_EPS).astype(BF16)

    def cols(n0):
        return jnp.dot(hn, w_ref[:, n0:n0 + tn], preferred_element_type=F32)

    if gated:
        third = w_ref.shape[1] // 3
        for n0 in range(0, third, tn):
            o_ref[:, n0:n0 + tn] = cols(n0).astype(o_ref.dtype)
            o_ref[:, third + n0:third + n0 + tn] = (
                cols(third + n0) * cols(2 * third + n0)).astype(o_ref.dtype)
    else:
        for n0 in range(0, o_ref.shape[1], tn):
            o_ref[:, n0:n0 + tn] = cols(n0).astype(o_ref.dtype)


def _norm_matmul(h, gain, w, layer, wlayer, out_dtype, plan, gated=False):
    tm, tn, _ = _PLANS[plan]
    m, d = h.shape
    n = w.shape[2] // 3 * 2 if gated else w.shape[2]
    return pl.pallas_call(
        functools.partial(_norm_matmul_kernel, tn=tn, gated=gated),
        grid=(m // tm,),
        in_specs=[
            pl.BlockSpec((tm, d), lambda i: (i, 0)),
            pl.BlockSpec((None, 1, d), lambda i: (layer, 0, 0)),
            pl.BlockSpec((None, d, w.shape[2]), lambda i: (wlayer, 0, 0),
                         pipeline_mode=pl.Buffered(1)),
        ],
        out_specs=pl.BlockSpec((tm, n), lambda i: (i, 0)),
        out_shape=jax.ShapeDtypeStruct((m, n), out_dtype),
        compiler_params=_params(("parallel",), plan),
        name="norm_matmul",
    )(h, gain, w)


def _cast_kernel(x_ref, o_ref):
    o_ref[...] = x_ref[...].astype(o_ref.dtype)


def _cast_layers(w, plan, first=0, count=None):
    rows = min(_PLANS[plan].tm, w.shape[1])
    n_layers, r, c = w.shape
    count = n_layers if count is None else count
    return pl.pallas_call(
        _cast_kernel,
        grid=(count, r // rows),
        in_specs=[pl.BlockSpec((None, rows, c), lambda l, i: (first + l, i, 0))],
        out_specs=pl.BlockSpec((None, rows, c), lambda l, i: (l, i, 0)),
        out_shape=jax.ShapeDtypeStruct((count, r, c), BF16),
        compiler_params=_params(("parallel", "parallel"), plan),
        name="cast_layers",
    )(w)


def _proj_residual_kernel(x_ref, w_ref, h_ref, o_ref, *, tn):
    x = x_ref[...]
    for n0 in range(0, o_ref.shape[1], tn):
        cols = slice(n0, n0 + tn)
        o_ref[:, cols] = h_ref[:, cols] + jnp.dot(x, w_ref[:, cols],
                                                  preferred_element_type=F32)


def _proj_residual(x, w, wlayer, h):
    tm, tn, _ = _PLANS["out_proj"]
    m, k = x.shape
    n = w.shape[2]
    return pl.pallas_call(
        functools.partial(_proj_residual_kernel, tn=tn),
        grid=(m // tm,),
        in_specs=[
            pl.BlockSpec((tm, k), lambda i: (i, 0)),
            pl.BlockSpec((None, k, n), lambda i: (wlayer, 0, 0),
                         pipeline_mode=pl.Buffered(1)),
            pl.BlockSpec((tm, n), lambda i: (i, 0)),
        ],
        out_specs=pl.BlockSpec((tm, n), lambda i: (i, 0)),
        out_shape=jax.ShapeDtypeStruct((m, n), F32),
        compiler_params=_params(("parallel",), "out_proj"),
        name="proj_residual",
    )(x, w, h)


def _ple_kernel(h_ref, g_ref, wg_ref, b_ref, p_ref, wp_ref, gf_ref, o_ref, *, final_norm):
    h = h_ref[...]
    hn = _rms_normalize(h, g_ref[...], RMS_EPS).astype(BF16)
    gate = _sigmoid(jnp.dot(hn, wg_ref[...], preferred_element_type=F32) + b_ref[...])
    proj = jnp.dot(p_ref[...].astype(BF16), wp_ref[...], preferred_element_type=F32)
    out = h + gate * proj
    if final_norm:
        out = _rms_normalize(out, gf_ref[...], RMS_EPS)
    o_ref[...] = out


def _ple(h, gain, w_gate, b_gate, p, w_proj, gain_f, layer, final_norm):
    tm = _PLANS["ple"].tm
    m, d = h.shape
    e = p.shape[2]
    nblk = m // tm
    return pl.pallas_call(
        functools.partial(_ple_kernel, final_norm=final_norm),
        grid=(nblk,),
        in_specs=[
            pl.BlockSpec((tm, d), lambda i: (i, 0)),
            pl.BlockSpec((None, 1, d), lambda i: (layer, 0, 0)),
            pl.BlockSpec((None, d, d), lambda i: (layer, 0, 0),
                         pipeline_mode=pl.Buffered(1)),
            pl.BlockSpec((None, 1, d), lambda i: (layer, 0, 0)),
            pl.BlockSpec((None, tm, e), lambda i: (layer, i, 0)),
            pl.BlockSpec((None, e, d), lambda i: (layer, 0, 0),
                         pipeline_mode=pl.Buffered(1)),
            pl.BlockSpec((1, d), lambda i: (0, 0)),
        ],
        out_specs=pl.BlockSpec((tm, d), lambda i: (i, 0)),
        out_shape=jax.ShapeDtypeStruct((m, d), F32),
        compiler_params=_params(("parallel",), "ple"),
        name="ple",
    )(h, gain, w_gate, b_gate, p, w_proj, gain_f)


def _rows(start, dil):
    if dil == 1:
        return pl.ds(pl.multiple_of(start, A_BLOCK), A_BLOCK)
    return pl.ds(start, A_BLOCK, stride=dil)


def _dilated_attention_kernel(*refs):
    n_q = A_Q_PER_KV
    q_refs = refs[:n_q]
    n_kv = 4 * len(A_BRANCHES)
    kv_refs = refs[n_q:n_q + n_kv]
    o_ref = refs[n_q + n_kv]
    m_ref, l_ref, acc_ref, bias_ref = refs[n_q + n_kv + 1:]
    blk = A_BLOCK
    rows = n_q * blk
    exp2_scale = HEAD_DIM ** -0.5 * math.log2(math.e)

    qi = lax.broadcasted_iota(jnp.int32, (rows, 2 * blk), 0) % blk
    kj = lax.broadcasted_iota(jnp.int32, (rows, 2 * blk), 1)
    band = (kj >= qi) & (kj <= qi + blk)
    bias_ref[0] = jnp.where(band, 0.0, -jnp.inf)
    bias_ref[1] = jnp.where(band & (kj >= blk), 0.0, -jnp.inf)
    first_bias = (pl.program_id(1) == 0).astype(jnp.int32)

    def scores(dil, start, kp_ref, kp_start, kc_ref, vp_ref, vc_ref, bias_idx):
        sel = _rows(start, dil)
        psel = _rows(kp_start, dil)
        q = jnp.concatenate([q_ref[sel, :] for q_ref in q_refs], axis=0).astype(BF16)
        k = jnp.concatenate([kp_ref[psel, :], kc_ref[sel, :]], axis=0).astype(BF16)
        v = jnp.concatenate([vp_ref[psel, :], vc_ref[sel, :]], axis=0).astype(BF16)
        s = lax.dot_general(q, k, (((1,), (1,)), ((), ())),
                            preferred_element_type=F32) + bias_ref[bias_idx]
        return s, v

    def softmax_pv(s, v):
        m_blk = jnp.max(s, axis=-1, keepdims=True)
        p = jnp.exp2((s - m_blk) * exp2_scale)
        l_blk = jnp.sum(p, axis=-1, keepdims=True)
        return m_blk, l_blk, jnp.dot(p.astype(BF16), v, preferred_element_type=F32)

    def merge(dil, mode, start, m_blk, l_blk, pv):
        sel = _rows(start, dil)
        if mode == "init":
            m_new = jnp.broadcast_to(m_blk, (rows, HEAD_DIM))
            l_new = jnp.broadcast_to(l_blk, (rows, HEAD_DIM))
            acc_new = pv
        else:
            m_old = jnp.concatenate([m_ref.at[h][sel, :] for h in range(n_q)], axis=0)
            l_old = jnp.concatenate([l_ref.at[h][sel, :] for h in range(n_q)], axis=0)
            acc_old = jnp.concatenate([acc_ref.at[h][sel, :] for h in range(n_q)], axis=0)
            m_new = jnp.maximum(m_old, m_blk)
            alpha = jnp.exp2((m_old - m_new) * exp2_scale)
            beta = jnp.exp2((m_blk - m_new) * exp2_scale)
            l_new = alpha * l_old + beta * l_blk
            acc_new = alpha * acc_old + beta * pv
        for h in range(n_q):
            r = slice(h * blk, (h + 1) * blk)
            if mode == "final":
                o_ref[sel, h * HEAD_DIM:(h + 1) * HEAD_DIM] = (
                    acc_new[r] / l_new[r]).astype(o_ref.dtype)
            else:
                m_ref.at[h][sel, :] = m_new[r]
                l_ref.at[h][sel, :] = l_new[r]
                acc_ref.at[h][sel, :] = acc_new[r]

    def attend(dil, mode, blocks):
        loaded = [scores(dil, *block) for block in blocks]
        stats = [softmax_pv(s, v) for s, v in loaded]
        for block, stat in zip(blocks, stats):
            merge(dil, mode, block[0], *stat)

    modes = {16: "init", 4: "update", 1: "final"}
    grp = A_BLOCKS_PER_TRIP
    for branch, (_, dil) in enumerate(A_BRANCHES):
        kc_ref, kp_ref, vc_ref, vp_ref = kv_refs[4 * branch:4 * branch + 4]
        span = dil * blk
        n_later = dil * (A_TILE // span - 1)
        mode = modes[dil]

        def first_block(r, kc_ref=kc_ref, kp_ref=kp_ref, vc_ref=vc_ref, vp_ref=vp_ref):
            return (r, kp_ref, r, kc_ref, vp_ref, vc_ref, first_bias)

        def later_block(i, dil=dil, span=span, kc_ref=kc_ref, vc_ref=vc_ref):
            start = i % dil + span * (i // dil + 1)
            return (start, kc_ref, start - span, kc_ref, vc_ref, vc_ref, 0)

        def first_group(i, carry, dil=dil, mode=mode, first_block=first_block):
            attend(dil, mode, [first_block(grp * i + g) for g in range(grp)])
            return carry

        def later_group(i, carry, offset, dil=dil, mode=mode, later_block=later_block):
            attend(dil, mode, [later_block(grp * i + g + offset) for g in range(grp)])
            return carry

        if dil > 1:
            lax.fori_loop(0, dil // grp, first_group, 0)
            if n_later:
                lax.fori_loop(0, n_later // grp, functools.partial(later_group, offset=0), 0)
        else:
            attend(dil, mode, [first_block(0)] + [later_block(g) for g in range(grp - 1)])
            lax.fori_loop(0, (n_later - grp + 1) // grp,
                          functools.partial(later_group, offset=grp - 1), 0)


def _dilated_attention(qkv, b, s):
    c = qkv.shape[-1]
    n_kv = (c // HEAD_DIM) // (A_Q_PER_KV + 2 * len(A_BRANCHES))
    n_qh = n_kv * A_Q_PER_KV
    d = n_qh * HEAD_DIM
    qkv = qkv.reshape(b, s, c)
    tiles = s // A_TILE

    def q_spec(h):
        return pl.BlockSpec((None, A_TILE, HEAD_DIM),
                            lambda bi, t, g: (bi, t, g * A_Q_PER_KV + h))

    def kv_specs(dil, is_v):
        col = n_qh + (2 * A_BRANCH_COLUMN[dil] + is_v) * n_kv
        ratio = A_TILE // (dil * A_BLOCK)
        cur = pl.BlockSpec((None, A_TILE, HEAD_DIM), lambda bi, t, g: (bi, t, col + g))
        prev = pl.BlockSpec((None, dil * A_BLOCK, HEAD_DIM),
                            lambda bi, t, g: (bi, jnp.maximum(t * ratio - 1, 0), col + g))
        return [cur, prev]

    in_specs = [q_spec(h) for h in range(A_Q_PER_KV)]
    for _, dil in A_BRANCHES:
        in_specs += kv_specs(dil, 0) + kv_specs(dil, 1)
    state = pltpu.VMEM((A_Q_PER_KV, A_TILE, HEAD_DIM), F32)
    out = pl.pallas_call(
        _dilated_attention_kernel,
        grid=(b, tiles, n_kv),
        in_specs=in_specs,
        out_specs=pl.BlockSpec((None, A_TILE, A_Q_PER_KV * HEAD_DIM),
                               lambda bi, t, g: (bi, t, g)),
        out_shape=jax.ShapeDtypeStruct((b, s, d), BF16),
        scratch_shapes=[state, state, state,
                        pltpu.VMEM((2, A_Q_PER_KV * A_BLOCK, 2 * A_BLOCK), F32)],
        compiler_params=_params(("parallel", "parallel", "parallel"), "dilated_attention"),
        name="dilated_attention",
    )(*([qkv] * len(in_specs)))
    return out.reshape(b * s, d)


def _diff_attention_kernel(q_ref, k_ref, v_ref, lam_ref, g_ref, o_ref,
                           vt_ref, acc_ref, sa_ref, sb_ref, *, tq, lambda_init):
    qi = pl.program_id(2)
    n_chunks = v_ref.shape[1] // tq
    exp2_scale = HEAD_DIM ** -0.5 * math.log2(math.e)

    @pl.when(qi == 0)
    def _():
        def transpose_chunk(kc, carry):
            start = pl.multiple_of(kc * tq, tq)
            vt_ref[:, pl.ds(start, tq)] = v_ref[0, pl.ds(start, tq), :].T
            return carry
        lax.fori_loop(0, n_chunks, transpose_chunk, 0)

    acc_ref[...] = jnp.zeros(acc_ref.shape, F32)
    q = q_ref[0]

    def scores_into(s_ref, kc):
        start = pl.multiple_of(kc * tq, tq)
        k = k_ref[0, pl.ds(start, tq), :]
        for c in range(2):
            lanes = slice(c * HEAD_DIM, (c + 1) * HEAD_DIM)
            s_ref[c] = lax.dot_general(k[:, lanes], q[:, lanes], (((1,), (1,)), ((), ())),
                                       preferred_element_type=F32)

    def attend(s_ref, kc, state, masked):
        start = pl.multiple_of(kc * tq, tq)
        vt = vt_ref[:, pl.ds(start, tq)]
        new_state = []
        for c in range(2):
            m_old, l_old = state[c]
            st = s_ref[c]
            if masked:
                key = lax.broadcasted_iota(jnp.int32, st.shape, 0)
                query = lax.broadcasted_iota(jnp.int32, st.shape, 1)
                st = jnp.where(key <= query, st, -jnp.inf)
            m_new = jnp.maximum(m_old, jnp.max(st, axis=0, keepdims=True))
            alpha = jnp.exp2((m_old - m_new) * exp2_scale)
            pt = jnp.exp2((st - m_new) * exp2_scale)
            l_new = alpha * l_old + jnp.sum(pt, axis=0, keepdims=True)
            acc_ref[c] = alpha * acc_ref[c] + jnp.dot(vt, pt.astype(BF16),
                                                      preferred_element_type=F32)
            new_state.append((m_new, l_new))
        return tuple(new_state)

    def chunk_pair(j, state):
        scores_into(sb_ref, 2 * j + 1)
        state = attend(sa_ref, 2 * j, state, False)
        scores_into(sa_ref, 2 * j + 2)
        return attend(sb_ref, 2 * j + 1, state, False)

    def even_tail(state):
        return attend(sa_ref, qi, state, True)

    def odd_tail(state):
        scores_into(sb_ref, qi)
        state = attend(sa_ref, qi - 1, state, False)
        return attend(sb_ref, qi, state, True)

    init = ((jnp.full((1, tq), -jnp.inf, F32), jnp.zeros((1, tq), F32)),) * 2
    scores_into(sa_ref, 0)
    state = lax.fori_loop(0, qi // 2, chunk_pair, init)
    (_, l0), (_, l1) = lax.cond(qi % 2 == 0, even_tail, odd_tail, state)

    lp = lam_ref[...]
    lam = (jnp.exp(jnp.sum(lp[0:1] * lp[1:2], axis=-1, keepdims=True))
           - jnp.exp(jnp.sum(lp[2:3] * lp[3:4], axis=-1, keepdims=True)) + lambda_init)
    ot = acc_ref[0] * (1.0 / l0) - lam * (acc_ref[1] * (1.0 / l1))
    o = _rms_normalize(ot.T, g_ref[...], SUBLN_EPS) * (1.0 - lambda_init)
    o_ref[0] = o.astype(o_ref.dtype)


def _diff_attention(qkv, lam_params, subln_g, wlayer, lambda_init, b, s):
    tq = _PLANS["diff_attention"].tm
    d = qkv.shape[-1] // 3
    hw = 2 * HEAD_DIM
    n_heads = d // hw
    qkv = qkv.reshape(b, s, 3 * d)
    out = pl.pallas_call(
        functools.partial(_diff_attention_kernel, tq=tq, lambda_init=lambda_init),
        grid=(b, n_heads, s // tq),
        in_specs=[
            pl.BlockSpec((1, tq, hw), lambda bi, h, i: (bi, i, h)),
            pl.BlockSpec((1, s, hw), lambda bi, h, i: (bi, 0, n_heads + h)),
            pl.BlockSpec((1, s, hw), lambda bi, h, i: (bi, 0, 2 * n_heads + h)),
            pl.BlockSpec((None, 4, HEAD_DIM), lambda bi, h, i: (wlayer, 0, 0)),
            pl.BlockSpec((None, 1, hw), lambda bi, h, i: (wlayer, 0, 0)),
        ],
        out_specs=pl.BlockSpec((1, tq, hw), lambda bi, h, i: (bi, i, h)),
        out_shape=jax.ShapeDtypeStruct((b, s, d), BF16),
        scratch_shapes=[pltpu.VMEM((hw, s), BF16), pltpu.VMEM((2, hw, tq), F32),
                        pltpu.VMEM((2, tq, tq), F32), pltpu.VMEM((2, tq, tq), F32)],
        compiler_params=_params(("parallel", "parallel", "arbitrary"), "diff_attention"),
        name="diff_attention",
    )(qkv, qkv, qkv, lam_params, subln_g)
    return out.reshape(b * s, d)


def _conv_proj_kernel(bg_ref, z_in_ref, zh_ref, cw_ref, w_ref, h_ref, o_ref,
                      z_ref, *, tiles_per_seq):
    tm = bg_ref.shape[0]
    halo = zh_ref.shape[0]
    seq_start = pl.program_id(0) % tiles_per_seq == 0
    z_ref[0:halo, :] = jnp.where(seq_start, 0.0, zh_ref[...])
    z_ref[halo:, :] = z_in_ref[...]
    cw = cw_ref[...]
    conv = (cw[0:1] * z_ref[pl.ds(halo - 2, tm), :]
            + cw[1:2] * z_ref[pl.ds(halo - 1, tm), :]
            + cw[2:3] * z_ref[pl.ds(halo, tm), :])
    y = (bg_ref[...] * conv).astype(BF16)
    o_ref[...] = h_ref[...] + jnp.dot(y, w_ref[...], preferred_element_type=F32)


def _conv_proj(proj, conv_w, w_out, wlayer, h, s):
    tm = _PLANS["conv_proj"].tm
    m, d = h.shape
    halo = F32_SUBLANES
    ratio = tm // halo
    return pl.pallas_call(
        functools.partial(_conv_proj_kernel, tiles_per_seq=s // tm),
        grid=(m // tm,),
        in_specs=[
            pl.BlockSpec((tm, d), lambda i: (i, 0)),
            pl.BlockSpec((tm, d), lambda i: (i, 1)),
            pl.BlockSpec((halo, d), lambda i: (jnp.maximum(i * ratio - 1, 0), 1)),
            pl.BlockSpec((None, 3, d), lambda i: (wlayer, 0, 0)),
            pl.BlockSpec((None, d, d), lambda i: (wlayer, 0, 0),
                         pipeline_mode=pl.Buffered(1)),
            pl.BlockSpec((tm, d), lambda i: (i, 0)),
        ],
        out_specs=pl.BlockSpec((tm, d), lambda i: (i, 0)),
        out_shape=jax.ShapeDtypeStruct((m, d), F32),
        scratch_shapes=[pltpu.VMEM((tm + halo, d), F32)],
        compiler_params=_params(("parallel",), "conv_proj"),
        name="conv_proj",
    )(proj, proj, proj, conv_w, w_out, h)


def kernel(x, p, norm_ffn1, w_ffn1_in, w_ffn1_out, norm_mix, a_w_qkv, a_w_o, b_w_qkv, b_w_o,
           b_lambda, b_subln, c_w_in, c_conv_w, c_w_out, norm_ffn2, w_ffn2_in, w_ffn2_out,
           norm_ple, w_ple_gate, b_ple_gate, w_ple_proj, norm_f):
    b, s, d = x.shape
    depth = p.shape[0]
    m = b * s
    h = x.reshape(m, d)
    p = p.reshape(depth, m, p.shape[-1])

    def row(t):
        return t.reshape(t.shape[0], 1, t.shape[-1])

    norm_ffn1, norm_mix, norm_ffn2, norm_ple, b_ple_gate, b_subln = (
        row(t) for t in (norm_ffn1, norm_mix, norm_ffn2, norm_ple, b_ple_gate, b_subln))
    norm_f = norm_f.reshape(1, d)
    a_w_qkv, a_w_o, b_w_qkv, b_w_o, c_w_in, c_w_out, w_ple_gate, w_ple_proj = (
        _cast_layers(t, "cast") for t in (a_w_qkv, a_w_o, b_w_qkv, b_w_o, c_w_in, c_w_out,
                                          w_ple_gate, w_ple_proj))
    w_in = _cast_layers(w_ffn1_in, "cast_ffn_in", 0, 1)
    w_out = _cast_layers(w_ffn1_out, "cast_ffn_out", 0, 1)

    for i in range(depth):
        h, w_in, w_out = _ffn(h, norm_ffn1, i, w_in, w_out, 0, (w_ffn2_in, w_ffn2_out, i))
        kind, j = i % N_MIXERS, i // N_MIXERS
        if kind == 0:
            qkv = _norm_matmul(h, norm_mix, a_w_qkv, i, j, F32, "proj_a")
            mix = _dilated_attention(qkv, b, s)
            h = _proj_residual(mix, a_w_o, j, h)
        elif kind == 1:
            lambda_init = 0.8 - 0.6 * math.exp(-0.3 * i)
            qkv = _norm_matmul(h, norm_mix, b_w_qkv, i, j, BF16, "proj_b")
            mix = _diff_attention(qkv, b_lambda, b_subln, j, lambda_init, b, s)
            h = _proj_residual(mix, b_w_o, j, h)
        else:
            proj = _norm_matmul(h, norm_mix, c_w_in, i, j, F32, "proj_c", gated=True)
            h = _conv_proj(proj, c_conv_w, c_w_out, j, h, s)
        if i + 1 < depth:
            h, w_in, w_out = _ffn(h, norm_ffn2, i, w_in, w_out, 0,
                                  (w_ffn1_in, w_ffn1_out, i + 1))
        else:
            h, = _ffn(h, norm_ffn2, i, w_in, w_out, 0, None)
        h = _ple(h, norm_ple, w_ple_gate, b_ple_gate, p, w_ple_proj, norm_f, i,
                 i == depth - 1)
    return h.reshape(b, s, d)
```

```python
import functools
import math
from typing import NamedTuple

import jax
import jax.numpy as jnp
from jax import lax
from jax.experimental import pallas as pl
from jax.experimental.pallas import tpu as pltpu

BF16 = jnp.bfloat16
F32 = jnp.float32

HEAD_DIM = 128
A_Q_PER_KV = 4
A_BRANCHES = ((2048, 16), (512, 4), (128, 1))
A_BRANCH_COLUMN = {1: 0, 4: 1, 16: 2}
A_BLOCK = 128
A_TILE = 2048
A_BLOCKS_PER_TRIP = {16: 2, 4: 4, 1: 4}
RMS_EPS = 1e-6
SUBLN_EPS = 1e-5
N_MIXERS = 3

MIB = 1024 * 1024
V7X_VMEM_MIB = 64
F32_SUBLANES = 8


class _Plan(NamedTuple):
    tm: int
    tn: int
    vmem_mib: int


_PLANS = {
    "ffn": _Plan(1024, 512, 60),
    "proj_a": _Plan(512, 1024, 60),
    "proj_b": _Plan(512, 1024, 56),
    "proj_c": _Plan(512, 1024, 60),
    "out_proj": _Plan(512, 1024, 48),
    "conv_proj": _Plan(512, 0, 56),
    "ple": _Plan(512, 0, 56),
    "diff_attention": _Plan(512, 0, 40),
    "dilated_attention": _Plan(A_TILE, 0, 56),
    "cast": _Plan(512, 0, 48),
    "cast_ffn_in": _Plan(128, 0, 32),
    "cast_ffn_out": _Plan(512, 0, 32),
}
assert all(plan.vmem_mib < V7X_VMEM_MIB for plan in _PLANS.values())


def _params(semantics, plan):
    return pltpu.CompilerParams(dimension_semantics=semantics,
                                vmem_limit_bytes=_PLANS[plan].vmem_mib * MIB)


def _rms_normalize(x, gain, eps):
    ms = jnp.mean(x * x, axis=-1, keepdims=True)
    return x * lax.rsqrt(ms + eps) * gain


def _sigmoid(x):
    return 1.0 / (1.0 + jnp.exp(-x))


def _ffn_kernel(*refs, cast_next):
    if cast_next:
        (h_ref, g_ref, wg_ref, wu_ref, wo_ref, next_in_ref, next_out_ref,
         o_ref, next_in_bf_ref, next_out_bf_ref, hn_ref) = refs
        next_in_bf_ref[...] = next_in_ref[...].astype(BF16)
        next_out_bf_ref[...] = next_out_ref[...].astype(BF16)
    else:
        h_ref, g_ref, wg_ref, wu_ref, wo_ref, o_ref, hn_ref = refs

    @pl.when(pl.program_id(1) == 0)
    def _():
        h = h_ref[...]
        hn_ref[...] = _rms_normalize(h, g_ref[...], RMS_EPS).astype(BF16)
        o_ref[...] = h

    hn = hn_ref[...]
    gate = jnp.dot(hn, wg_ref[...], preferred_element_type=F32)
    up = jnp.dot(hn, wu_ref[...], preferred_element_type=F32)
    act = (0.5 * gate * _sigmoid(gate)) * up
    o_ref[...] += jnp.dot(act.astype(BF16), wo_ref[...], preferred_element_type=F32)


def _ffn(h, gain, layer, w_in, w_out, wlayer, next_w):
    tm, tf = _PLANS["ffn"][:2]
    m, d = h.shape
    f = w_out.shape[1]
    nf = f // tf
    n_i = m // tm
    in_specs = [
        pl.BlockSpec((tm, d), lambda i, j: (i, 0)),
        pl.BlockSpec((None, 1, d), lambda i, j: (layer, 0, 0)),
        pl.BlockSpec((None, d, tf), lambda i, j: (wlayer, 0, j)),
        pl.BlockSpec((None, d, tf), lambda i, j: (wlayer, 0, nf + j)),
        pl.BlockSpec((None, tf, d), lambda i, j: (wlayer, j, 0)),
    ]
    out_specs = [pl.BlockSpec((tm, d), lambda i, j: (i, 0))]
    out_shape = [jax.ShapeDtypeStruct((m, d), F32)]
    args = [h, gain, w_in, w_in, w_out]
    if next_w is not None:
        next_in, next_out, nl = next_w
        out_rows = f // (n_i * nf)
        in_specs += [pl.BlockSpec((None, d // n_i, 2 * tf), lambda i, j: (nl, i, j)),
                     pl.BlockSpec((None, out_rows, d), lambda i, j: (nl, i * nf + j, 0))]
        out_specs += [pl.BlockSpec((None, d // n_i, 2 * tf), lambda i, j: (0, i, j)),
                      pl.BlockSpec((None, out_rows, d), lambda i, j: (0, i * nf + j, 0))]
        out_shape += [jax.ShapeDtypeStruct((1, d, 2 * f), BF16),
                      jax.ShapeDtypeStruct((1, f, d), BF16)]
        args += [next_in, next_out]
    return pl.pallas_call(
        functools.partial(_ffn_kernel, cast_next=next_w is not None),
        grid=(n_i, nf),
        in_specs=in_specs,
        out_specs=out_specs,
        out_shape=out_shape,
        scratch_shapes=[pltpu.VMEM((tm, d), BF16)],
        compiler_params=_params(("parallel", "arbitrary"), "ffn"),
        name="ffn",
    )(*args)


def _norm_matmul_kernel(h_ref, g_ref, w_ref, o_ref, *, tn, gated):
    hn = _rms_normalize(h_ref[...], g_ref[...], RMS_EPS).astype(BF16)

    def cols(n0):
        return jnp.dot(hn, w_ref[:, n0:n0 + tn], preferred_element_type=F32)

    if gated:
        third = w_ref.shape[1] // 3
        for n0 in range(0, third, tn):
            o_ref[:, n0:n0 + tn] = cols(n0).astype(o_ref.dtype)
            o_ref[:, third + n0:third + n0 + tn] = (
                cols(third + n0) * cols(2 * third + n0)).astype(o_ref.dtype)
    else:
        for n0 in range(0, o_ref.shape[1], tn):
            o_ref[:, n0:n0 + tn] = cols(n0).astype(o_ref.dtype)


def _norm_matmul(h, gain, w, layer, wlayer, out_dtype, plan, gated=False):
    tm, tn = _PLANS[plan][:2]
    m, d = h.shape
    n = w.shape[2] // 3 * 2 if gated else w.shape[2]
    return pl.pallas_call(
        functools.partial(_norm_matmul_kernel, tn=tn, gated=gated),
        grid=(m // tm,),
        in_specs=[
            pl.BlockSpec((tm, d), lambda i: (i, 0)),
            pl.BlockSpec((None, 1, d), lambda i: (layer, 0, 0)),
            pl.BlockSpec((None, d, w.shape[2]), lambda i: (wlayer, 0, 0),
                         pipeline_mode=pl.Buffered(1)),
        ],
        out_specs=pl.BlockSpec((tm, n), lambda i: (i, 0)),
        out_shape=jax.ShapeDtypeStruct((m, n), out_dtype),
        compiler_params=_params(("parallel",), plan),
        name="norm_matmul",
    )(h, gain, w)


def _cast_kernel(x_ref, o_ref):
    o_ref[...] = x_ref[...].astype(o_ref.dtype)


def _cast_layers(w, plan, first=0, count=None):
    rows = min(_PLANS[plan].tm, w.shape[1])
    n_layers, r, c = w.shape
    count = n_layers if count is None else count
    return pl.pallas_call(
        _cast_kernel,
        grid=(count, r // rows),
        in_specs=[pl.BlockSpec((None, rows, c), lambda l, i: (first + l, i, 0))],
        out_specs=pl.BlockSpec((None, rows, c), lambda l, i: (l, i, 0)),
        out_shape=jax.ShapeDtypeStruct((count, r, c), BF16),
        compiler_params=_params(("parallel", "parallel"), plan),
        name="cast_layers",
    )(w)


def _proj_residual_kernel(x_ref, w_ref, h_ref, o_ref, *, tn):
    x = x_ref[...]
    for n0 in range(0, o_ref.shape[1], tn):
        cols = slice(n0, n0 + tn)
        o_ref[:, cols] = h_ref[:, cols] + jnp.dot(x, w_ref[:, cols],
                                                  preferred_element_type=F32)


def _proj_residual(x, w, wlayer, h):
    tm, tn = _PLANS["out_proj"][:2]
    m, k = x.shape
    n = w.shape[2]
    return pl.pallas_call(
        functools.partial(_proj_residual_kernel, tn=tn),
        grid=(m // tm,),
        in_specs=[
            pl.BlockSpec((tm, k), lambda i: (i, 0)),
            pl.BlockSpec((None, k, n), lambda i: (wlayer, 0, 0),
                         pipeline_mode=pl.Buffered(1)),
            pl.BlockSpec((tm, n), lambda i: (i, 0)),
        ],
        out_specs=pl.BlockSpec((tm, n), lambda i: (i, 0)),
        out_shape=jax.ShapeDtypeStruct((m, n), F32),
        compiler_params=_params(("parallel",), "out_proj"),
        name="proj_residual",
    )(x, w, h)


def _ple_kernel(h_ref, g_ref, wg_ref, b_ref, p_ref, wp_ref, gf_ref, o_ref, *, final_norm):
    h = h_ref[...]
    hn = _rms_normalize(h, g_ref[...], RMS_EPS).astype(BF16)
    gate = _sigmoid(jnp.dot(hn, wg_ref[...], preferred_element_type=F32) + b_ref[...])
    proj = jnp.dot(p_ref[...].astype(BF16), wp_ref[...], preferred_element_type=F32)
    out = h + gate * proj
    if final_norm:
        out = _rms_normalize(out, gf_ref[...], RMS_EPS)
    o_ref[...] = out


def _ple(h, gain, w_gate, b_gate, p, w_proj, gain_f, layer, final_norm):
    tm = _PLANS["ple"].tm
    m, d = h.shape
    e = p.shape[2]
    nblk = m // tm
    return pl.pallas_call(
        functools.partial(_ple_kernel, final_norm=final_norm),
        grid=(nblk,),
        in_specs=[
            pl.BlockSpec((tm, d), lambda i: (i, 0)),
            pl.BlockSpec((None, 1, d), lambda i: (layer, 0, 0)),
            pl.BlockSpec((None, d, d), lambda i: (layer, 0, 0),
                         pipeline_mode=pl.Buffered(1)),
            pl.BlockSpec((None, 1, d), lambda i: (layer, 0, 0)),
            pl.BlockSpec((None, tm, e), lambda i: (layer, i, 0)),
            pl.BlockSpec((None, e, d), lambda i: (layer, 0, 0),
                         pipeline_mode=pl.Buffered(1)),
            pl.BlockSpec((1, d), lambda i: (0, 0)),
        ],
        out_specs=pl.BlockSpec((tm, d), lambda i: (i, 0)),
        out_shape=jax.ShapeDtypeStruct((m, d), F32),
        compiler_params=_params(("parallel",), "ple"),
        name="ple",
    )(h, gain, w_gate, b_gate, p, w_proj, gain_f)


def _rows(start, dil):
    if dil == 1:
        return pl.ds(pl.multiple_of(start, A_BLOCK), A_BLOCK)
    return pl.ds(start, A_BLOCK, stride=dil)


def _dilated_attention_kernel(*refs):
    n_q = A_Q_PER_KV
    q_refs = refs[:n_q]
    n_kv = 4 * len(A_BRANCHES)
    kv_refs = refs[n_q:n_q + n_kv]
    o_ref = refs[n_q + n_kv]
    m_ref, l_ref, acc_ref, bias_ref = refs[n_q + n_kv + 1:]
    blk = A_BLOCK
    rows = n_q * blk
    exp2_scale = HEAD_DIM ** -0.5 * math.log2(math.e)

    qi = lax.broadcasted_iota(jnp.int32, (rows, 2 * blk), 0) % blk
    kj = lax.broadcasted_iota(jnp.int32, (rows, 2 * blk), 1)
    band = (kj >= qi) & (kj <= qi + blk)
    bias_ref[0] = jnp.where(band, 0.0, -jnp.inf)
    bias_ref[1] = jnp.where(band & (kj >= blk), 0.0, -jnp.inf)
    first_bias = (pl.program_id(1) == 0).astype(jnp.int32)

    def scores(dil, start, kp_ref, kp_start, kc_ref, vp_ref, vc_ref, bias_idx):
        sel = _rows(start, dil)
        psel = _rows(kp_start, dil)
        q = jnp.concatenate([q_ref[sel, :] for q_ref in q_refs], axis=0).astype(BF16)
        k = jnp.concatenate([kp_ref[psel, :], kc_ref[sel, :]], axis=0).astype(BF16)
        v = jnp.concatenate([vp_ref[psel, :], vc_ref[sel, :]], axis=0).astype(BF16)
        s = lax.dot_general(q, k, (((1,), (1,)), ((), ())),
                            preferred_element_type=F32) + bias_ref[bias_idx]
        return s, v

    def softmax_pv(s, v):
        m_blk = jnp.max(s, axis=-1, keepdims=True)
        p = jnp.exp2((s - m_blk) * exp2_scale)
        l_blk = jnp.sum(p, axis=-1, keepdims=True)
        return m_blk, l_blk, jnp.dot(p.astype(BF16), v, preferred_element_type=F32)

    def merge(dil, mode, start, m_blk, l_blk, pv):
        sel = _rows(start, dil)
        if mode == "init":
            m_new = jnp.broadcast_to(m_blk, (rows, HEAD_DIM))
            l_new = jnp.broadcast_to(l_blk, (rows, HEAD_DIM))
            acc_new = pv
        else:
            m_old = jnp.concatenate([m_ref.at[h][sel, :] for h in range(n_q)], axis=0)
            l_old = jnp.concatenate([l_ref.at[h][sel, :] for h in range(n_q)], axis=0)
            acc_old = jnp.concatenate([acc_ref.at[h][sel, :] for h in range(n_q)], axis=0)
            m_new = jnp.maximum(m_old, m_blk)
            alpha = jnp.exp2((m_old - m_new) * exp2_scale)
            beta = jnp.exp2((m_blk - m_new) * exp2_scale)
            l_new = alpha * l_old + beta * l_blk
            acc_new = alpha * acc_old + beta * pv
        for h in range(n_q):
            r = slice(h * blk, (h + 1) * blk)
            if mode == "final":
                o_ref[sel, h * HEAD_DIM:(h + 1) * HEAD_DIM] = (
                    acc_new[r] / l_new[r]).astype(o_ref.dtype)
            else:
                m_ref.at[h][sel, :] = m_new[r]
                l_ref.at[h][sel, :] = l_new[r]
                acc_ref.at[h][sel, :] = acc_new[r]

    def attend(dil, mode, blocks):
        loaded = [scores(dil, *block) for block in blocks]
        stats = [softmax_pv(s, v) for s, v in loaded]
        for block, stat in zip(blocks, stats):
            merge(dil, mode, block[0], *stat)

    modes = {16: "init", 4: "update", 1: "final"}
    for branch, (_, dil) in enumerate(A_BRANCHES):
        grp = A_BLOCKS_PER_TRIP[dil]
        kc_ref, kp_ref, vc_ref, vp_ref = kv_refs[4 * branch:4 * branch + 4]
        span = dil * blk
        n_later = dil * (A_TILE // span - 1)
        mode = modes[dil]

        def first_block(r, kc_ref=kc_ref, kp_ref=kp_ref, vc_ref=vc_ref, vp_ref=vp_ref):
            return (r, kp_ref, r, kc_ref, vp_ref, vc_ref, first_bias)

        def later_block(i, dil=dil, span=span, kc_ref=kc_ref, vc_ref=vc_ref):
            start = i % dil + span * (i // dil + 1)
            return (start, kc_ref, start - span, kc_ref, vc_ref, vc_ref, 0)

        def first_group(i, carry, dil=dil, mode=mode, grp=grp, first_block=first_block):
            attend(dil, mode, [first_block(grp * i + g) for g in range(grp)])
            return carry

        def later_group(i, carry, offset, dil=dil, mode=mode, grp=grp,
                        later_block=later_block):
            attend(dil, mode, [later_block(grp * i + g + offset) for g in range(grp)])
            return carry

        if dil > 1:
            lax.fori_loop(0, dil // grp, first_group, 0)
            if n_later:
                lax.fori_loop(0, n_later // grp, functools.partial(later_group, offset=0), 0)
        else:
            attend(dil, mode, [first_block(0)] + [later_block(g) for g in range(grp - 1)])
            lax.fori_loop(0, (n_later - grp + 1) // grp,
                          functools.partial(later_group, offset=grp - 1), 0)


def _dilated_attention(qkv, b, s):
    c = qkv.shape[-1]
    n_kv = (c // HEAD_DIM) // (A_Q_PER_KV + 2 * len(A_BRANCHES))
    n_qh = n_kv * A_Q_PER_KV
    d = n_qh * HEAD_DIM
    qkv = qkv.reshape(b, s, c)
    tiles = s // A_TILE

    def q_spec(h):
        return pl.BlockSpec((None, A_TILE, HEAD_DIM),
                            lambda bi, t, g: (bi, t, g * A_Q_PER_KV + h))

    def kv_specs(dil, is_v):
        col = n_qh + (2 * A_BRANCH_COLUMN[dil] + is_v) * n_kv
        ratio = A_TILE // (dil * A_BLOCK)
        cur = pl.BlockSpec((None, A_TILE, HEAD_DIM), lambda bi, t, g: (bi, t, col + g))
        prev = pl.BlockSpec((None, dil * A_BLOCK, HEAD_DIM),
                            lambda bi, t, g: (bi, jnp.maximum(t * ratio - 1, 0), col + g))
        return [cur, prev]

    in_specs = [q_spec(h) for h in range(A_Q_PER_KV)]
    for _, dil in A_BRANCHES:
        in_specs += kv_specs(dil, 0) + kv_specs(dil, 1)
    state = pltpu.VMEM((A_Q_PER_KV, A_TILE, HEAD_DIM), F32)
    out = pl.pallas_call(
        _dilated_attention_kernel,
        grid=(b, tiles, n_kv),
        in_specs=in_specs,
        out_specs=pl.BlockSpec((None, A_TILE, A_Q_PER_KV * HEAD_DIM),
                               lambda bi, t, g: (bi, t, g)),
        out_shape=jax.ShapeDtypeStruct((b, s, d), BF16),
        scratch_shapes=[state, state, state,
                        pltpu.VMEM((2, A_Q_PER_KV * A_BLOCK, 2 * A_BLOCK), F32)],
        compiler_params=_params(("parallel", "parallel", "parallel"), "dilated_attention"),
        name="dilated_attention",
    )(*([qkv] * len(in_specs)))
    return out.reshape(b * s, d)


def _diff_attention_kernel(q_ref, k_ref, v_ref, lam_ref, g_ref, o_ref,
                           vt_ref, acc_ref, sa_ref, sb_ref, *, tq, lambda_init):
    qi = pl.program_id(2)
    n_chunks = v_ref.shape[1] // tq
    exp2_scale = HEAD_DIM ** -0.5 * math.log2(math.e)

    @pl.when(qi == 0)
    def _():
        def transpose_chunk(kc, carry):
            start = pl.multiple_of(kc * tq, tq)
            vt_ref[:, pl.ds(start, tq)] = v_ref[0, pl.ds(start, tq), :].T
            return carry
        lax.fori_loop(0, n_chunks, transpose_chunk, 0)

    acc_ref[...] = jnp.zeros(acc_ref.shape, F32)
    q = q_ref[0]

    def scores_into(s_ref, kc):
        start = pl.multiple_of(kc * tq, tq)
        k = k_ref[0, pl.ds(start, tq), :]
        for c in range(2):
            lanes = slice(c * HEAD_DIM, (c + 1) * HEAD_DIM)
            s_ref[c] = lax.dot_general(k[:, lanes], q[:, lanes], (((1,), (1,)), ((), ())),
                                       preferred_element_type=F32)

    def attend(s_ref, kc, state, masked):
        start = pl.multiple_of(kc * tq, tq)
        vt = vt_ref[:, pl.ds(start, tq)]
        new_state = []
        for c in range(2):
            m_old, l_old = state[c]
            st = s_ref[c]
            if masked:
                key = lax.broadcasted_iota(jnp.int32, st.shape, 0)
                query = lax.broadcasted_iota(jnp.int32, st.shape, 1)
                st = jnp.where(key <= query, st, -jnp.inf)
            m_new = jnp.maximum(m_old, jnp.max(st, axis=0, keepdims=True))
            alpha = jnp.exp2((m_old - m_new) * exp2_scale)
            pt = jnp.exp2((st - m_new) * exp2_scale)
            l_new = alpha * l_old + jnp.sum(pt, axis=0, keepdims=True)
            acc_ref[c] = alpha * acc_ref[c] + jnp.dot(vt, pt.astype(BF16),
                                                      preferred_element_type=F32)
            new_state.append((m_new, l_new))
        return tuple(new_state)

    def chunk_pair(j, state):
        scores_into(sb_ref, 2 * j + 1)
        state = attend(sa_ref, 2 * j, state, False)
        scores_into(sa_ref, 2 * j + 2)
        return attend(sb_ref, 2 * j + 1, state, False)

    def even_tail(state):
        return attend(sa_ref, qi, state, True)

    def odd_tail(state):
        scores_into(sb_ref, qi)
        state = attend(sa_ref, qi - 1, state, False)
        return attend(sb_ref, qi, state, True)

    init = ((jnp.full((1, tq), -jnp.inf, F32), jnp.zeros((1, tq), F32)),) * 2
    scores_into(sa_ref, 0)
    state = lax.fori_loop(0, qi // 4,
                          lambda j, st: chunk_pair(2 * j + 1, chunk_pair(2 * j, st)), init)
    state = lax.fori_loop(2 * (qi // 4), qi // 2, chunk_pair, state)
    (_, l0), (_, l1) = lax.cond(qi % 2 == 0, even_tail, odd_tail, state)

    lp = lam_ref[...]
    lam = (jnp.exp(jnp.sum(lp[0:1] * lp[1:2], axis=-1, keepdims=True))
           - jnp.exp(jnp.sum(lp[2:3] * lp[3:4], axis=-1, keepdims=True)) + lambda_init)
    ot = acc_ref[0] * (1.0 / l0) - lam * (acc_ref[1] * (1.0 / l1))
    o = _rms_normalize(ot.T, g_ref[...], SUBLN_EPS) * (1.0 - lambda_init)
    o_ref[0] = o.astype(o_ref.dtype)


def _diff_attention(qkv, lam_params, subln_g, wlayer, lambda_init, b, s):
    tq = _PLANS["diff_attention"].tm
    d = qkv.shape[-1] // 3
    hw = 2 * HEAD_DIM
    n_heads = d // hw
    qkv = qkv.reshape(b, s, 3 * d)
    out = pl.pallas_call(
        functools.partial(_diff_attention_kernel, tq=tq, lambda_init=lambda_init),
        grid=(b, n_heads, s // tq),
        in_specs=[
            pl.BlockSpec((1, tq, hw), lambda bi, h, i: (bi, i, h)),
            pl.BlockSpec((1, s, hw), lambda bi, h, i: (bi, 0, n_heads + h)),
            pl.BlockSpec((1, s, hw), lambda bi, h, i: (bi, 0, 2 * n_heads + h)),
            pl.BlockSpec((None, 4, HEAD_DIM), lambda bi, h, i: (wlayer, 0, 0)),
            pl.BlockSpec((None, 1, hw), lambda bi, h, i: (wlayer, 0, 0)),
        ],
        out_specs=pl.BlockSpec((1, tq, hw), lambda bi, h, i: (bi, i, h)),
        out_shape=jax.ShapeDtypeStruct((b, s, d), BF16),
        scratch_shapes=[pltpu.VMEM((hw, s), BF16), pltpu.VMEM((2, hw, tq), F32),
                        pltpu.VMEM((2, tq, tq), F32), pltpu.VMEM((2, tq, tq), F32)],
        compiler_params=_params(("parallel", "parallel", "arbitrary"), "diff_attention"),
        name="diff_attention",
    )(qkv, qkv, qkv, lam_params, subln_g)
    return out.reshape(b * s, d)


def _conv_proj_kernel(bg_ref, z_in_ref, zh_ref, cw_ref, w_ref, h_ref, o_ref,
                      z_ref, *, tiles_per_seq):
    tm = bg_ref.shape[0]
    halo = zh_ref.shape[0]
    seq_start = pl.program_id(0) % tiles_per_seq == 0
    z_ref[0:halo, :] = jnp.where(seq_start, 0.0, zh_ref[...])
    z_ref[halo:, :] = z_in_ref[...]
    cw = cw_ref[...]
    conv = (cw[0:1] * z_ref[pl.ds(halo - 2, tm), :]
            + cw[1:2] * z_ref[pl.ds(halo - 1, tm), :]
            + cw[2:3] * z_ref[pl.ds(halo, tm), :])
    y = (bg_ref[...] * conv).astype(BF16)
    o_ref[...] = h_ref[...] + jnp.dot(y, w_ref[...], preferred_element_type=F32)


def _conv_proj(proj, conv_w, w_out, wlayer, h, s):
    tm = _PLANS["conv_proj"].tm
    m, d = h.shape
    halo = F32_SUBLANES
    ratio = tm // halo
    return pl.pallas_call(
        functools.partial(_conv_proj_kernel, tiles_per_seq=s // tm),
        grid=(m // tm,),
        in_specs=[
            pl.BlockSpec((tm, d), lambda i: (i, 0)),
            pl.BlockSpec((tm, d), lambda i: (i, 1)),
            pl.BlockSpec((halo, d), lambda i: (jnp.maximum(i * ratio - 1, 0), 1)),
            pl.BlockSpec((None, 3, d), lambda i: (wlayer, 0, 0)),
            pl.BlockSpec((None, d, d), lambda i: (wlayer, 0, 0),
                         pipeline_mode=pl.Buffered(1)),
            pl.BlockSpec((tm, d), lambda i: (i, 0)),
        ],
        out_specs=pl.BlockSpec((tm, d), lambda i: (i, 0)),
        out_shape=jax.ShapeDtypeStruct((m, d), F32),
        scratch_shapes=[pltpu.VMEM((tm + halo, d), F32)],
        compiler_params=_params(("parallel",), "conv_proj"),
        name="conv_proj",
    )(proj, proj, proj, conv_w, w_out, h)


def kernel(x, p, norm_ffn1, w_ffn1_in, w_ffn1_out, norm_mix, a_w_qkv, a_w_o, b_w_qkv, b_w_o,
           b_lambda, b_subln, c_w_in, c_conv_w, c_w_out, norm_ffn2, w_ffn2_in, w_ffn2_out,
           norm_ple, w_ple_gate, b_ple_gate, w_ple_proj, norm_f):
    b, s, d = x.shape
    depth = p.shape[0]
    m = b * s
    h = x.reshape(m, d)
    p = p.reshape(depth, m, p.shape[-1])

    def row(t):
        return t.reshape(t.shape[0], 1, t.shape[-1])

    norm_ffn1, norm_mix, norm_ffn2, norm_ple, b_ple_gate, b_subln = (
        row(t) for t in (norm_ffn1, norm_mix, norm_ffn2, norm_ple, b_ple_gate, b_subln))
    norm_f = norm_f.reshape(1, d)
    a_w_qkv, a_w_o, b_w_qkv, b_w_o, c_w_in, c_w_out, w_ple_gate, w_ple_proj = (
        _cast_layers(t, "cast") for t in (a_w_qkv, a_w_o, b_w_qkv, b_w_o, c_w_in, c_w_out,
                                          w_ple_gate, w_ple_proj))
    w_in = _cast_layers(w_ffn1_in, "cast_ffn_in", 0, 1)
    w_out = _cast_layers(w_ffn1_out, "cast_ffn_out", 0, 1)

    for i in range(depth):
        h, w_in, w_out = _ffn(h, norm_ffn1, i, w_in, w_out, 0, (w_ffn2_in, w_ffn2_out, i))
        kind, j = i % N_MIXERS, i // N_MIXERS
        if kind == 0:
            qkv = _norm_matmul(h, norm_mix, a_w_qkv, i, j, F32, "proj_a")
            mix = _dilated_attention(qkv, b, s)
            h = _proj_residual(mix, a_w_o, j, h)
        elif kind == 1:
            lambda_init = 0.8 - 0.6 * math.exp(-0.3 * i)
            qkv = _norm_matmul(h, norm_mix, b_w_qkv, i, j, BF16, "proj_b")
            mix = _diff_attention(qkv, b_lambda, b_subln, j, lambda_init, b, s)
            h = _proj_residual(mix, b_w_o, j, h)
        else:
            proj = _norm_matmul(h, norm_mix, c_w_in, i, j, F32, "proj_c", gated=True)
            h = _conv_proj(proj, c_conv_w, c_w_out, j, h, s)
        if i + 1 < depth:
            h, w_in, w_out = _ffn(h, norm_ffn2, i, w_in, w_out, 0,
                                  (w_ffn1_in, w_ffn1_out, i + 1))
        else:
            h, = _ffn(h, norm_ffn2, i, w_in, w_out, 0, None)
        h = _ple(h, norm_ple, w_ple_gate, b_ple_gate, p, w_ple_proj, norm_f, i,
                 i == depth - 1)
    return h.reshape(b, s, d)
```

```python
import functools
import math
from typing import NamedTuple

import jax
import jax.numpy as jnp
from jax import lax
from jax.experimental import pallas as pl
from jax.experimental.pallas import tpu as pltpu

BF16 = jnp.bfloat16
F32 = jnp.float32

HEAD_DIM = 128
A_Q_PER_KV = 4
A_BRANCHES = ((2048, 16), (512, 4), (128, 1))
A_BRANCH_COLUMN = {1: 0, 4: 1, 16: 2}
A_BLOCK = 128
A_TILE = 2048
A_BLOCKS_PER_TRIP = {16: 4, 4: 4, 1: 8}
RMS_EPS = 1e-6
SUBLN_EPS = 1e-5
N_MIXERS = 3

MIB = 1024 * 1024
V7X_VMEM_MIB = 64
F32_SUBLANES = 8


class _Plan(NamedTuple):
    tm: int
    tn: int
    vmem_mib: int


_PLANS = {
    "ffn": _Plan(1024, 512, 60),
    "proj_a": _Plan(512, 1024, 60),
    "proj_b": _Plan(512, 1024, 56),
    "proj_c": _Plan(512, 1024, 60),
    "out_proj": _Plan(512, 1024, 48),
    "conv_proj": _Plan(512, 0, 56),
    "ple": _Plan(512, 0, 56),
    "diff_attention": _Plan(512, 0, 40),
    "dilated_attention": _Plan(A_TILE, 0, 56),
    "cast": _Plan(512, 0, 48),
    "cast_ffn_in": _Plan(128, 0, 32),
    "cast_ffn_out": _Plan(512, 0, 32),
}
assert all(plan.vmem_mib < V7X_VMEM_MIB for plan in _PLANS.values())


def _params(semantics, plan):
    return pltpu.CompilerParams(dimension_semantics=semantics,
                                vmem_limit_bytes=_PLANS[plan].vmem_mib * MIB)


def _rms_normalize(x, gain, eps):
    ms = jnp.mean(x * x, axis=-1, keepdims=True)
    return x * lax.rsqrt(ms + eps) * gain


def _sigmoid(x):
    return 1.0 / (1.0 + jnp.exp(-x))


def _ffn_kernel(*refs, cast_next):
    if cast_next:
        (h_ref, g_ref, wg_ref, wu_ref, wo_ref, next_in_ref, next_out_ref,
         o_ref, next_in_bf_ref, next_out_bf_ref, hn_ref) = refs
        next_in_bf_ref[...] = next_in_ref[...].astype(BF16)
        next_out_bf_ref[...] = next_out_ref[...].astype(BF16)
    else:
        h_ref, g_ref, wg_ref, wu_ref, wo_ref, o_ref, hn_ref = refs

    @pl.when(pl.program_id(1) == 0)
    def _():
        h = h_ref[...]
        hn_ref[...] = _rms_normalize(h, g_ref[...], RMS_EPS).astype(BF16)
        o_ref[...] = h

    hn = hn_ref[...]
    gate = jnp.dot(hn, wg_ref[...], preferred_element_type=F32)
    up = jnp.dot(hn, wu_ref[...], preferred_element_type=F32)
    act = (0.5 * gate * _sigmoid(gate)) * up
    o_ref[...] += jnp.dot(act.astype(BF16), wo_ref[...], preferred_element_type=F32)


def _ffn(h, gain, layer, w_in, w_out, wlayer, next_w):
    tm, tf = _PLANS["ffn"][:2]
    m, d = h.shape
    f = w_out.shape[1]
    nf = f // tf
    n_i = m // tm
    in_specs = [
        pl.BlockSpec((tm, d), lambda i, j: (i, 0)),
        pl.BlockSpec((None, 1, d), lambda i, j: (layer, 0, 0)),
        pl.BlockSpec((None, d, tf), lambda i, j: (wlayer, 0, j)),
        pl.BlockSpec((None, d, tf), lambda i, j: (wlayer, 0, nf + j)),
        pl.BlockSpec((None, tf, d), lambda i, j: (wlayer, j, 0)),
    ]
    out_specs = [pl.BlockSpec((tm, d), lambda i, j: (i, 0))]
    out_shape = [jax.ShapeDtypeStruct((m, d), F32)]
    args = [h, gain, w_in, w_in, w_out]
    if next_w is not None:
        next_in, next_out, nl = next_w
        out_rows = f // (n_i * nf)
        in_specs += [pl.BlockSpec((None, d // n_i, 2 * tf), lambda i, j: (nl, i, j)),
                     pl.BlockSpec((None, out_rows, d), lambda i, j: (nl, i * nf + j, 0))]
        out_specs += [pl.BlockSpec((None, d // n_i, 2 * tf), lambda i, j: (0, i, j)),
                      pl.BlockSpec((None, out_rows, d), lambda i, j: (0, i * nf + j, 0))]
        out_shape += [jax.ShapeDtypeStruct((1, d, 2 * f), BF16),
                      jax.ShapeDtypeStruct((1, f, d), BF16)]
        args += [next_in, next_out]
    return pl.pallas_call(
        functools.partial(_ffn_kernel, cast_next=next_w is not None),
        grid=(n_i, nf),
        in_specs=in_specs,
        out_specs=out_specs,
        out_shape=out_shape,
        scratch_shapes=[pltpu.VMEM((tm, d), BF16)],
        compiler_params=_params(("parallel", "arbitrary"), "ffn"),
        name="ffn",
    )(*args)


def _norm_matmul_kernel(h_ref, g_ref, w_ref, o_ref, *, tn, gated):
    hn = _rms_normalize(h_ref[...], g_ref[...], RMS_EPS).astype(BF16)

    def cols(n0):
        return jnp.dot(hn, w_ref[:, n0:n0 + tn], preferred_element_type=F32)

    if gated:
        third = w_ref.shape[1] // 3
        for n0 in range(0, third, tn):
            o_ref[:, n0:n0 + tn] = cols(n0).astype(o_ref.dtype)
            o_ref[:, third + n0:third + n0 + tn] = (
                cols(third + n0) * cols(2 * third + n0)).astype(o_ref.dtype)
    else:
        for n0 in range(0, o_ref.shape[1], tn):
            o_ref[:, n0:n0 + tn] = cols(n0).astype(o_ref.dtype)


def _norm_matmul(h, gain, w, layer, wlayer, out_dtype, plan, gated=False):
    tm, tn = _PLANS[plan][:2]
    m, d = h.shape
    n = w.shape[2] // 3 * 2 if gated else w.shape[2]
    return pl.pallas_call(
        functools.partial(_norm_matmul_kernel, tn=tn, gated=gated),
        grid=(m // tm,),
        in_specs=[
            pl.BlockSpec((tm, d), lambda i: (i, 0)),
            pl.BlockSpec((None, 1, d), lambda i: (layer, 0, 0)),
            pl.BlockSpec((None, d, w.shape[2]), lambda i: (wlayer, 0, 0),
                         pipeline_mode=pl.Buffered(1)),
        ],
        out_specs=pl.BlockSpec((tm, n), lambda i: (i, 0)),
        out_shape=jax.ShapeDtypeStruct((m, n), out_dtype),
        compiler_params=_params(("parallel",), plan),
        name="norm_matmul",
    )(h, gain, w)


def _cast_kernel(x_ref, o_ref):
    o_ref[...] = x_ref[...].astype(o_ref.dtype)


def _cast_layers(w, plan, first=0, count=None):
    rows = min(_PLANS[plan].tm, w.shape[1])
    n_layers, r, c = w.shape
    count = n_layers if count is None else count
    return pl.pallas_call(
        _cast_kernel,
        grid=(count, r // rows),
        in_specs=[pl.BlockSpec((None, rows, c), lambda l, i: (first + l, i, 0))],
        out_specs=pl.BlockSpec((None, rows, c), lambda l, i: (l, i, 0)),
        out_shape=jax.ShapeDtypeStruct((count, r, c), BF16),
        compiler_params=_params(("parallel", "parallel"), plan),
        name="cast_layers",
    )(w)


def _proj_residual_kernel(x_ref, w_ref, h_ref, o_ref, *, tn):
    x = x_ref[...]
    for n0 in range(0, o_ref.shape[1], tn):
        cols = slice(n0, n0 + tn)
        o_ref[:, cols] = h_ref[:, cols] + jnp.dot(x, w_ref[:, cols],
                                                  preferred_element_type=F32)


def _proj_residual(x, w, wlayer, h):
    tm, tn = _PLANS["out_proj"][:2]
    m, k = x.shape
    n = w.shape[2]
    return pl.pallas_call(
        functools.partial(_proj_residual_kernel, tn=tn),
        grid=(m // tm,),
        in_specs=[
            pl.BlockSpec((tm, k), lambda i: (i, 0)),
            pl.BlockSpec((None, k, n), lambda i: (wlayer, 0, 0),
                         pipeline_mode=pl.Buffered(1)),
            pl.BlockSpec((tm, n), lambda i: (i, 0)),
        ],
        out_specs=pl.BlockSpec((tm, n), lambda i: (i, 0)),
        out_shape=jax.ShapeDtypeStruct((m, n), F32),
        compiler_params=_params(("parallel",), "out_proj"),
        name="proj_residual",
    )(x, w, h)


def _ple_kernel(h_ref, g_ref, wg_ref, b_ref, p_ref, wp_ref, gf_ref, o_ref, *, final_norm):
    h = h_ref[...]
    hn = _rms_normalize(h, g_ref[...], RMS_EPS).astype(BF16)
    gate = _sigmoid(jnp.dot(hn, wg_ref[...], preferred_element_type=F32) + b_ref[...])
    proj = jnp.dot(p_ref[...].astype(BF16), wp_ref[...], preferred_element_type=F32)
    out = h + gate * proj
    if final_norm:
        out = _rms_normalize(out, gf_ref[...], RMS_EPS)
    o_ref[...] = out


def _ple(h, gain, w_gate, b_gate, p, w_proj, gain_f, layer, final_norm):
    tm = _PLANS["ple"].tm
    m, d = h.shape
    e = p.shape[2]
    nblk = m // tm
    return pl.pallas_call(
        functools.partial(_ple_kernel, final_norm=final_norm),
        grid=(nblk,),
        in_specs=[
            pl.BlockSpec((tm, d), lambda i: (i, 0)),
            pl.BlockSpec((None, 1, d), lambda i: (layer, 0, 0)),
            pl.BlockSpec((None, d, d), lambda i: (layer, 0, 0),
                         pipeline_mode=pl.Buffered(1)),
            pl.BlockSpec((None, 1, d), lambda i: (layer, 0, 0)),
            pl.BlockSpec((None, tm, e), lambda i: (layer, i, 0)),
            pl.BlockSpec((None, e, d), lambda i: (layer, 0, 0),
                         pipeline_mode=pl.Buffered(1)),
            pl.BlockSpec((1, d), lambda i: (0, 0)),
        ],
        out_specs=pl.BlockSpec((tm, d), lambda i: (i, 0)),
        out_shape=jax.ShapeDtypeStruct((m, d), F32),
        compiler_params=_params(("parallel",), "ple"),
        name="ple",
    )(h, gain, w_gate, b_gate, p, w_proj, gain_f)


def _rows(start, dil):
    if dil == 1:
        return pl.ds(pl.multiple_of(start, A_BLOCK), A_BLOCK)
    return pl.ds(start, A_BLOCK, stride=dil)


def _dilated_attention_kernel(*refs):
    n_q = A_Q_PER_KV
    q_refs = refs[:n_q]
    n_kv = 4 * len(A_BRANCHES)
    kv_refs = refs[n_q:n_q + n_kv]
    o_ref = refs[n_q + n_kv]
    m_ref, l_ref, acc_ref, bias_ref = refs[n_q + n_kv + 1:]
    blk = A_BLOCK
    rows = n_q * blk
    exp2_scale = HEAD_DIM ** -0.5 * math.log2(math.e)

    qi = lax.broadcasted_iota(jnp.int32, (rows, 2 * blk), 0) % blk
    kj = lax.broadcasted_iota(jnp.int32, (rows, 2 * blk), 1)
    band = (kj >= qi) & (kj <= qi + blk)
    bias_ref[0] = jnp.where(band, 0.0, -jnp.inf)
    bias_ref[1] = jnp.where(band & (kj >= blk), 0.0, -jnp.inf)
    first_bias = (pl.program_id(1) == 0).astype(jnp.int32)

    def scores(dil, start, kp_ref, kp_start, kc_ref, vp_ref, vc_ref, bias_idx):
        sel = _rows(start, dil)
        psel = _rows(kp_start, dil)
        q = jnp.concatenate([q_ref[sel, :] for q_ref in q_refs], axis=0).astype(BF16)
        k = jnp.concatenate([kp_ref[psel, :], kc_ref[sel, :]], axis=0).astype(BF16)
        v = jnp.concatenate([vp_ref[psel, :], vc_ref[sel, :]], axis=0).astype(BF16)
        s = lax.dot_general(q, k, (((1,), (1,)), ((), ())),
                            preferred_element_type=F32) + bias_ref[bias_idx]
        return s, v

    def softmax_pv(s, v):
        m_blk = jnp.max(s, axis=-1, keepdims=True)
        p = jnp.exp2((s - m_blk) * exp2_scale)
        l_blk = jnp.sum(p, axis=-1, keepdims=True)
        return m_blk, l_blk, jnp.dot(p.astype(BF16), v, preferred_element_type=F32)

    def merge(dil, mode, start, m_blk, l_blk, pv):
        sel = _rows(start, dil)
        if mode == "init":
            m_new = jnp.broadcast_to(m_blk, (rows, HEAD_DIM))
            l_new = jnp.broadcast_to(l_blk, (rows, HEAD_DIM))
            acc_new = pv
        else:
            m_old = jnp.concatenate([m_ref.at[h][sel, :] for h in range(n_q)], axis=0)
            l_old = jnp.concatenate([l_ref.at[h][sel, :] for h in range(n_q)], axis=0)
            acc_old = jnp.concatenate([acc_ref.at[h][sel, :] for h in range(n_q)], axis=0)
            m_new = jnp.maximum(m_old, m_blk)
            alpha = jnp.exp2((m_old - m_new) * exp2_scale)
            beta = jnp.exp2((m_blk - m_new) * exp2_scale)
            l_new = alpha * l_old + beta * l_blk
            acc_new = alpha * acc_old + beta * pv
        for h in range(n_q):
            r = slice(h * blk, (h + 1) * blk)
            if mode == "final":
                o_ref[sel, h * HEAD_DIM:(h + 1) * HEAD_DIM] = (
                    acc_new[r] / l_new[r]).astype(o_ref.dtype)
            else:
                m_ref.at[h][sel, :] = m_new[r]
                l_ref.at[h][sel, :] = l_new[r]
                acc_ref.at[h][sel, :] = acc_new[r]

    def attend(dil, mode, blocks):
        loaded = [scores(dil, *block) for block in blocks]
        stats = [softmax_pv(s, v) for s, v in loaded]
        for block, stat in zip(blocks, stats):
            merge(dil, mode, block[0], *stat)

    modes = {16: "init", 4: "update", 1: "final"}
    for branch, (_, dil) in enumerate(A_BRANCHES):
        grp = A_BLOCKS_PER_TRIP[dil]
        kc_ref, kp_ref, vc_ref, vp_ref = kv_refs[4 * branch:4 * branch + 4]
        span = dil * blk
        n_later = dil * (A_TILE // span - 1)
        mode = modes[dil]

        def first_block(r, kc_ref=kc_ref, kp_ref=kp_ref, vc_ref=vc_ref, vp_ref=vp_ref):
            return (r, kp_ref, r, kc_ref, vp_ref, vc_ref, first_bias)

        def later_block(i, dil=dil, span=span, kc_ref=kc_ref, vc_ref=vc_ref):
            start = i % dil + span * (i // dil + 1)
            return (start, kc_ref, start - span, kc_ref, vc_ref, vc_ref, 0)

        def first_group(i, carry, dil=dil, mode=mode, grp=grp, first_block=first_block):
            attend(dil, mode, [first_block(grp * i + g) for g in range(grp)])
            return carry

        def later_group(i, carry, offset, dil=dil, mode=mode, grp=grp,
                        later_block=later_block):
            attend(dil, mode, [later_block(grp * i + g + offset) for g in range(grp)])
            return carry

        if dil > 1:
            lax.fori_loop(0, dil // grp, first_group, 0)
            if n_later:
                lax.fori_loop(0, n_later // grp, functools.partial(later_group, offset=0), 0)
        else:
            attend(dil, mode, [first_block(0)] + [later_block(g) for g in range(grp - 1)])
            lax.fori_loop(0, (n_later - grp + 1) // grp,
                          functools.partial(later_group, offset=grp - 1), 0)


def _dilated_attention(qkv, b, s):
    c = qkv.shape[-1]
    n_kv = (c // HEAD_DIM) // (A_Q_PER_KV + 2 * len(A_BRANCHES))
    n_qh = n_kv * A_Q_PER_KV
    d = n_qh * HEAD_DIM
    qkv = qkv.reshape(b, s, c)
    tiles = s // A_TILE

    def q_spec(h):
        return pl.BlockSpec((None, A_TILE, HEAD_DIM),
                            lambda bi, t, g: (bi, t, g * A_Q_PER_KV + h))

    def kv_specs(dil, is_v):
        col = n_qh + (2 * A_BRANCH_COLUMN[dil] + is_v) * n_kv
        ratio = A_TILE // (dil * A_BLOCK)
        cur = pl.BlockSpec((None, A_TILE, HEAD_DIM), lambda bi, t, g: (bi, t, col + g))
        prev = pl.BlockSpec((None, dil * A_BLOCK, HEAD_DIM),
                            lambda bi, t, g: (bi, jnp.maximum(t * ratio - 1, 0), col + g))
        return [cur, prev]

    in_specs = [q_spec(h) for h in range(A_Q_PER_KV)]
    for _, dil in A_BRANCHES:
        in_specs += kv_specs(dil, 0) + kv_specs(dil, 1)
    state = pltpu.VMEM((A_Q_PER_KV, A_TILE, HEAD_DIM), F32)
    out = pl.pallas_call(
        _dilated_attention_kernel,
        grid=(b, tiles, n_kv),
        in_specs=in_specs,
        out_specs=pl.BlockSpec((None, A_TILE, A_Q_PER_KV * HEAD_DIM),
                               lambda bi, t, g: (bi, t, g)),
        out_shape=jax.ShapeDtypeStruct((b, s, d), BF16),
        scratch_shapes=[state, state, state,
                        pltpu.VMEM((2, A_Q_PER_KV * A_BLOCK, 2 * A_BLOCK), F32)],
        compiler_params=_params(("parallel", "parallel", "parallel"), "dilated_attention"),
        name="dilated_attention",
    )(*([qkv] * len(in_specs)))
    return out.reshape(b * s, d)


def _diff_attention_kernel(q_ref, k_ref, v_ref, lam_ref, g_ref, o_ref,
                           vt_ref, acc_ref, sa_ref, sb_ref, *, tq, lambda_init):
    qi = pl.program_id(2)
    n_chunks = v_ref.shape[1] // tq
    exp2_scale = HEAD_DIM ** -0.5 * math.log2(math.e)

    @pl.when(qi == 0)
    def _():
        def transpose_chunk(kc, carry):
            start = pl.multiple_of(kc * tq, tq)
            vt_ref[:, pl.ds(start, tq)] = v_ref[0, pl.ds(start, tq), :].T
            return carry
        lax.fori_loop(0, n_chunks, transpose_chunk, 0)

    acc_ref[...] = jnp.zeros(acc_ref.shape, F32)
    q = q_ref[0]

    def scores_into(s_ref, kc):
        start = pl.multiple_of(kc * tq, tq)
        k = k_ref[0, pl.ds(start, tq), :]
        for c in range(2):
            lanes = slice(c * HEAD_DIM, (c + 1) * HEAD_DIM)
            s_ref[c] = lax.dot_general(k[:, lanes], q[:, lanes], (((1,), (1,)), ((), ())),
                                       preferred_element_type=F32)

    def attend(s_ref, kc, state, masked):
        start = pl.multiple_of(kc * tq, tq)
        vt = vt_ref[:, pl.ds(start, tq)]
        new_state = []
        for c in range(2):
            m_old, l_old = state[c]
            st = s_ref[c]
            if masked:
                key = lax.broadcasted_iota(jnp.int32, st.shape, 0)
                query = lax.broadcasted_iota(jnp.int32, st.shape, 1)
                st = jnp.where(key <= query, st, -jnp.inf)
            m_new = jnp.maximum(m_old, jnp.max(st, axis=0, keepdims=True))
            alpha = jnp.exp2((m_old - m_new) * exp2_scale)
            pt = jnp.exp2((st - m_new) * exp2_scale)
            l_new = alpha * l_old + jnp.sum(pt, axis=0, keepdims=True)
            acc_ref[c] = alpha * acc_ref[c] + jnp.dot(vt, pt.astype(BF16),
                                                      preferred_element_type=F32)
            new_state.append((m_new, l_new))
        return tuple(new_state)

    def chunk_pair(j, state):
        scores_into(sb_ref, 2 * j + 1)
        state = attend(sa_ref, 2 * j, state, False)
        scores_into(sa_ref, 2 * j + 2)
        return attend(sb_ref, 2 * j + 1, state, False)

    def even_tail(state):
        return attend(sa_ref, qi, state, True)

    def odd_tail(state):
        scores_into(sb_ref, qi)
        state = attend(sa_ref, qi - 1, state, False)
        return attend(sb_ref, qi, state, True)

    init = ((jnp.full((1, tq), -jnp.inf, F32), jnp.zeros((1, tq), F32)),) * 2
    scores_into(sa_ref, 0)
    def chunk_pairs(n):
        def body(j, st):
            for t in range(n):
                st = chunk_pair(n * j + t, st)
            return st
        return body

    state = lax.fori_loop(0, qi // 8, chunk_pairs(4), init)
    state = lax.fori_loop(2 * (qi // 8), qi // 4, chunk_pairs(2), state)
    state = lax.fori_loop(2 * (qi // 4), qi // 2, chunk_pair, state)
    (_, l0), (_, l1) = lax.cond(qi % 2 == 0, even_tail, odd_tail, state)

    lp = lam_ref[...]
    lam = (jnp.exp(jnp.sum(lp[0:1] * lp[1:2], axis=-1, keepdims=True))
           - jnp.exp(jnp.sum(lp[2:3] * lp[3:4], axis=-1, keepdims=True)) + lambda_init)
    ot = acc_ref[0] * (1.0 / l0) - lam * (acc_ref[1] * (1.0 / l1))
    o = _rms_normalize(ot.T, g_ref[...], SUBLN_EPS) * (1.0 - lambda_init)
    o_ref[0] = o.astype(o_ref.dtype)


def _diff_attention(qkv, lam_params, subln_g, wlayer, lambda_init, b, s):
    tq = _PLANS["diff_attention"].tm
    d = qkv.shape[-1] // 3
    hw = 2 * HEAD_DIM
    n_heads = d // hw
    qkv = qkv.reshape(b, s, 3 * d)
    out = pl.pallas_call(
        functools.partial(_diff_attention_kernel, tq=tq, lambda_init=lambda_init),
        grid=(b, n_heads, s // tq),
        in_specs=[
            pl.BlockSpec((1, tq, hw), lambda bi, h, i: (bi, i, h)),
            pl.BlockSpec((1, s, hw), lambda bi, h, i: (bi, 0, n_heads + h)),
            pl.BlockSpec((1, s, hw), lambda bi, h, i: (bi, 0, 2 * n_heads + h)),
            pl.BlockSpec((None, 4, HEAD_DIM), lambda bi, h, i: (wlayer, 0, 0)),
            pl.BlockSpec((None, 1, hw), lambda bi, h, i: (wlayer, 0, 0)),
        ],
        out_specs=pl.BlockSpec((1, tq, hw), lambda bi, h, i: (bi, i, h)),
        out_shape=jax.ShapeDtypeStruct((b, s, d), BF16),
        scratch_shapes=[pltpu.VMEM((hw, s), BF16), pltpu.VMEM((2, hw, tq), F32),
                        pltpu.VMEM((2, tq, tq), F32), pltpu.VMEM((2, tq, tq), F32)],
        compiler_params=_params(("parallel", "parallel", "arbitrary"), "diff_attention"),
        name="diff_attention",
    )(qkv, qkv, qkv, lam_params, subln_g)
    return out.reshape(b * s, d)


def _conv_proj_kernel(bg_ref, z_in_ref, zh_ref, cw_ref, w_ref, h_ref, o_ref,
                      z_ref, *, tiles_per_seq):
    tm = bg_ref.shape[0]
    halo = zh_ref.shape[0]
    seq_start = pl.program_id(0) % tiles_per_seq == 0
    z_ref[0:halo, :] = jnp.where(seq_start, 0.0, zh_ref[...])
    z_ref[halo:, :] = z_in_ref[...]
    cw = cw_ref[...]
    conv = (cw[0:1] * z_ref[pl.ds(halo - 2, tm), :]
            + cw[1:2] * z_ref[pl.ds(halo - 1, tm), :]
            + cw[2:3] * z_ref[pl.ds(halo, tm), :])
    y = (bg_ref[...] * conv).astype(BF16)
    o_ref[...] = h_ref[...] + jnp.dot(y, w_ref[...], preferred_element_type=F32)


def _conv_proj(proj, conv_w, w_out, wlayer, h, s):
    tm = _PLANS["conv_proj"].tm
    m, d = h.shape
    halo = F32_SUBLANES
    ratio = tm // halo
    return pl.pallas_call(
        functools.partial(_conv_proj_kernel, tiles_per_seq=s // tm),
        grid=(m // tm,),
        in_specs=[
            pl.BlockSpec((tm, d), lambda i: (i, 0)),
            pl.BlockSpec((tm, d), lambda i: (i, 1)),
            pl.BlockSpec((halo, d), lambda i: (jnp.maximum(i * ratio - 1, 0), 1)),
            pl.BlockSpec((None, 3, d), lambda i: (wlayer, 0, 0)),
            pl.BlockSpec((None, d, d), lambda i: (wlayer, 0, 0),
                         pipeline_mode=pl.Buffered(1)),
            pl.BlockSpec((tm, d), lambda i: (i, 0)),
        ],
        out_specs=pl.BlockSpec((tm, d), lambda i: (i, 0)),
        out_shape=jax.ShapeDtypeStruct((m, d), F32),
        scratch_shapes=[pltpu.VMEM((tm + halo, d), F32)],
        compiler_params=_params(("parallel",), "conv_proj"),
        name="conv_proj",
    )(proj, proj, proj, conv_w, w_out, h)


def kernel(x, p, norm_ffn1, w_ffn1_in, w_ffn1_out, norm_mix, a_w_qkv, a_w_o, b_w_qkv, b_w_o,
           b_lambda, b_subln, c_w_in, c_conv_w, c_w_out, norm_ffn2, w_ffn2_in, w_ffn2_out,
           norm_ple, w_ple_gate, b_ple_gate, w_ple_proj, norm_f):
    b, s, d = x.shape
    depth = p.shape[0]
    m = b * s
    h = x.reshape(m, d)
    p = p.reshape(depth, m, p.shape[-1])

    def row(t):
        return t.reshape(t.shape[0], 1, t.shape[-1])

    norm_ffn1, norm_mix, norm_ffn2, norm_ple, b_ple_gate, b_subln = (
        row(t) for t in (norm_ffn1, norm_mix, norm_ffn2, norm_ple, b_ple_gate, b_subln))
    norm_f = norm_f.reshape(1, d)
    a_w_qkv, a_w_o, b_w_qkv, b_w_o, c_w_in, c_w_out, w_ple_gate, w_ple_proj = (
        _cast_layers(t, "cast") for t in (a_w_qkv, a_w_o, b_w_qkv, b_w_o, c_w_in, c_w_out,
                                          w_ple_gate, w_ple_proj))
    w_in = _cast_layers(w_ffn1_in, "cast_ffn_in", 0, 1)
    w_out = _cast_layers(w_ffn1_out, "cast_ffn_out", 0, 1)

    for i in range(depth):
        h, w_in, w_out = _ffn(h, norm_ffn1, i, w_in, w_out, 0, (w_ffn2_in, w_ffn2_out, i))
        kind, j = i % N_MIXERS, i // N_MIXERS
        if kind == 0:
            qkv = _norm_matmul(h, norm_mix, a_w_qkv, i, j, F32, "proj_a")
            mix = _dilated_attention(qkv, b, s)
            h = _proj_residual(mix, a_w_o, j, h)
        elif kind == 1:
            lambda_init = 0.8 - 0.6 * math.exp(-0.3 * i)
            qkv = _norm_matmul(h, norm_mix, b_w_qkv, i, j, BF16, "proj_b")
            mix = _diff_attention(qkv, b_lambda, b_subln, j, lambda_init, b, s)
            h = _proj_residual(mix, b_w_o, j, h)
        else:
            proj = _norm_matmul(h, norm_mix, c_w_in, i, j, F32, "proj_c", gated=True)
            h = _conv_proj(proj, c_conv_w, c_w_out, j, h, s)
        if i + 1 < depth:
            h, w_in, w_out = _ffn(h, norm_ffn2, i, w_in, w_out, 0,
                                  (w_ffn1_in, w_ffn1_out, i + 1))
        else:
            h, = _ffn(h, norm_ffn2, i, w_in, w_out, 0, None)
        h = _ple(h, norm_ple, w_ple_gate, b_ple_gate, p, w_ple_proj, norm_f, i,
                 i == depth - 1)
    return h.reshape(b, s, d)
```

```python
import functools
import math
from typing import NamedTuple

import jax
import jax.numpy as jnp
from jax import lax
from jax.experimental import pallas as pl
from jax.experimental.pallas import tpu as pltpu

BF16 = jnp.bfloat16
F32 = jnp.float32

HEAD_DIM = 128
A_Q_PER_KV = 4
A_BRANCHES = ((2048, 16), (512, 4), (128, 1))
A_BRANCH_COLUMN = {1: 0, 4: 1, 16: 2}
A_BLOCK = 128
A_TILE = 2048
A_BLOCKS_PER_TRIP = {16: 4, 4: 4, 1: 8}
RMS_EPS = 1e-6
SUBLN_EPS = 1e-5
N_MIXERS = 3

MIB = 1024 * 1024
V7X_VMEM_MIB = 64
F32_SUBLANES = 8


class _Plan(NamedTuple):
    tm: int
    tn: int
    vmem_mib: int


_PLANS = {
    "ffn": _Plan(1024, 512, 60),
    "proj_a": _Plan(512, 1024, 60),
    "proj_b": _Plan(512, 1024, 56),
    "proj_c": _Plan(512, 1024, 60),
    "out_proj": _Plan(512, 1024, 48),
    "conv_proj": _Plan(512, 0, 56),
    "ple": _Plan(512, 0, 56),
    "diff_attention": _Plan(512, 0, 40),
    "dilated_attention": _Plan(A_TILE, 0, 56),
    "cast_many": _Plan(32, 0, 40),
    "cast_ffn_in": _Plan(128, 0, 32),
    "cast_ffn_out": _Plan(512, 0, 32),
}
assert all(plan.vmem_mib < V7X_VMEM_MIB for plan in _PLANS.values())


def _params(semantics, plan):
    return pltpu.CompilerParams(dimension_semantics=semantics,
                                vmem_limit_bytes=_PLANS[plan].vmem_mib * MIB)


def _rms_normalize(x, gain, eps):
    ms = jnp.mean(x * x, axis=-1, keepdims=True)
    return x * lax.rsqrt(ms + eps) * gain


def _sigmoid(x):
    return 1.0 / (1.0 + jnp.exp(-x))


def _ffn_kernel(*refs, cast_next):
    if cast_next:
        (h_ref, g_ref, wg_ref, wu_ref, wo_ref, next_in_ref, next_out_ref,
         o_ref, next_in_bf_ref, next_out_bf_ref, hn_ref) = refs
        next_in_bf_ref[...] = next_in_ref[...].astype(BF16)
        next_out_bf_ref[...] = next_out_ref[...].astype(BF16)
    else:
        h_ref, g_ref, wg_ref, wu_ref, wo_ref, o_ref, hn_ref = refs

    @pl.when(pl.program_id(1) == 0)
    def _():
        h = h_ref[...]
        hn_ref[...] = _rms_normalize(h, g_ref[...], RMS_EPS).astype(BF16)
        o_ref[...] = h

    hn = hn_ref[...]
    gate = jnp.dot(hn, wg_ref[...], preferred_element_type=F32)
    up = jnp.dot(hn, wu_ref[...], preferred_element_type=F32)
    act = (0.5 * gate * _sigmoid(gate)) * up
    o_ref[...] += jnp.dot(act.astype(BF16), wo_ref[...], preferred_element_type=F32)


def _ffn(h, gain, layer, w_in, w_out, wlayer, next_w):
    tm, tf = _PLANS["ffn"][:2]
    m, d = h.shape
    f = w_out.shape[1]
    nf = f // tf
    n_i = m // tm
    in_specs = [
        pl.BlockSpec((tm, d), lambda i, j: (i, 0)),
        pl.BlockSpec((None, 1, d), lambda i, j: (layer, 0, 0)),
        pl.BlockSpec((None, d, tf), lambda i, j: (wlayer, 0, j)),
        pl.BlockSpec((None, d, tf), lambda i, j: (wlayer, 0, nf + j)),
        pl.BlockSpec((None, tf, d), lambda i, j: (wlayer, j, 0)),
    ]
    out_specs = [pl.BlockSpec((tm, d), lambda i, j: (i, 0))]
    out_shape = [jax.ShapeDtypeStruct((m, d), F32)]
    args = [h, gain, w_in, w_in, w_out]
    if next_w is not None:
        next_in, next_out, nl = next_w
        out_rows = f // (n_i * nf)
        in_specs += [pl.BlockSpec((None, d // n_i, 2 * tf), lambda i, j: (nl, i, j)),
                     pl.BlockSpec((None, out_rows, d), lambda i, j: (nl, i * nf + j, 0))]
        out_specs += [pl.BlockSpec((None, d // n_i, 2 * tf), lambda i, j: (0, i, j)),
                      pl.BlockSpec((None, out_rows, d), lambda i, j: (0, i * nf + j, 0))]
        out_shape += [jax.ShapeDtypeStruct((1, d, 2 * f), BF16),
                      jax.ShapeDtypeStruct((1, f, d), BF16)]
        args += [next_in, next_out]
    return pl.pallas_call(
        functools.partial(_ffn_kernel, cast_next=next_w is not None),
        grid=(n_i, nf),
        in_specs=in_specs,
        out_specs=out_specs,
        out_shape=out_shape,
        scratch_shapes=[pltpu.VMEM((tm, d), BF16)],
        compiler_params=_params(("parallel", "arbitrary"), "ffn"),
        name="ffn",
    )(*args)


def _norm_matmul_kernel(h_ref, g_ref, w_ref, o_ref, *, tn, gated):
    hn = _rms_normalize(h_ref[...], g_ref[...], RMS_EPS).astype(BF16)

    def cols(n0):
        return jnp.dot(hn, w_ref[:, n0:n0 + tn], preferred_element_type=F32)

    if gated:
        third = w_ref.shape[1] // 3
        for n0 in range(0, third, tn):
            o_ref[:, n0:n0 + tn] = cols(n0).astype(o_ref.dtype)
            o_ref[:, third + n0:third + n0 + tn] = (
                cols(third + n0) * cols(2 * third + n0)).astype(o_ref.dtype)
    else:
        for n0 in range(0, o_ref.shape[1], tn):
            o_ref[:, n0:n0 + tn] = cols(n0).astype(o_ref.dtype)


def _norm_matmul(h, gain, w, layer, wlayer, out_dtype, plan, gated=False):
    tm, tn = _PLANS[plan][:2]
    m, d = h.shape
    n = w.shape[2] // 3 * 2 if gated else w.shape[2]
    return pl.pallas_call(
        functools.partial(_norm_matmul_kernel, tn=tn, gated=gated),
        grid=(m // tm,),
        in_specs=[
            pl.BlockSpec((tm, d), lambda i: (i, 0)),
            pl.BlockSpec((None, 1, d), lambda i: (layer, 0, 0)),
            pl.BlockSpec((None, d, w.shape[2]), lambda i: (wlayer, 0, 0),
                         pipeline_mode=pl.Buffered(1)),
        ],
        out_specs=pl.BlockSpec((tm, n), lambda i: (i, 0)),
        out_shape=jax.ShapeDtypeStruct((m, n), out_dtype),
        compiler_params=_params(("parallel",), plan),
        name="norm_matmul",
    )(h, gain, w)


def _cast_kernel(x_ref, o_ref):
    o_ref[...] = x_ref[...].astype(o_ref.dtype)


def _cast_layers(w, plan, first=0, count=None):
    rows = min(_PLANS[plan].tm, w.shape[1])
    n_layers, r, c = w.shape
    count = n_layers if count is None else count
    return pl.pallas_call(
        _cast_kernel,
        grid=(count, r // rows),
        in_specs=[pl.BlockSpec((None, rows, c), lambda l, i: (first + l, i, 0))],
        out_specs=pl.BlockSpec((None, rows, c), lambda l, i: (l, i, 0)),
        out_shape=jax.ShapeDtypeStruct((count, r, c), BF16),
        compiler_params=_params(("parallel", "parallel"), plan),
        name="cast_layers",
    )(w)


def _cast_many_kernel(*refs):
    n = len(refs) // 2
    for x_ref, o_ref in zip(refs[:n], refs[n:]):
        o_ref[...] = x_ref[...].astype(o_ref.dtype)


def _cast_many(ws):
    steps = _PLANS["cast_many"].tm
    flat = [w.reshape(-1, w.shape[-1]) for w in ws]
    specs = [pl.BlockSpec((w.shape[0] // steps, w.shape[1]), lambda s: (s, 0)) for w in flat]
    outs = pl.pallas_call(
        _cast_many_kernel,
        grid=(steps,),
        in_specs=specs,
        out_specs=specs,
        out_shape=[jax.ShapeDtypeStruct(w.shape, BF16) for w in flat],
        compiler_params=_params(("parallel",), "cast_many"),
        name="cast_many",
    )(*flat)
    return [o.reshape(w.shape) for o, w in zip(outs, ws)]


def _proj_residual_kernel(x_ref, w_ref, h_ref, o_ref, *, tn):
    x = x_ref[...]
    for n0 in range(0, o_ref.shape[1], tn):
        cols = slice(n0, n0 + tn)
        o_ref[:, cols] = h_ref[:, cols] + jnp.dot(x, w_ref[:, cols],
                                                  preferred_element_type=F32)


def _proj_residual(x, w, wlayer, h):
    tm, tn = _PLANS["out_proj"][:2]
    m, k = x.shape
    n = w.shape[2]
    return pl.pallas_call(
        functools.partial(_proj_residual_kernel, tn=tn),
        grid=(m // tm,),
        in_specs=[
            pl.BlockSpec((tm, k), lambda i: (i, 0)),
            pl.BlockSpec((None, k, n), lambda i: (wlayer, 0, 0),
                         pipeline_mode=pl.Buffered(1)),
            pl.BlockSpec((tm, n), lambda i: (i, 0)),
        ],
        out_specs=pl.BlockSpec((tm, n), lambda i: (i, 0)),
        out_shape=jax.ShapeDtypeStruct((m, n), F32),
        compiler_params=_params(("parallel",), "out_proj"),
        name="proj_residual",
    )(x, w, h)


def _ple_kernel(h_ref, g_ref, wg_ref, b_ref, p_ref, wp_ref, gf_ref, o_ref, *, final_norm):
    h = h_ref[...]
    hn = _rms_normalize(h, g_ref[...], RMS_EPS).astype(BF16)
    gate = _sigmoid(jnp.dot(hn, wg_ref[...], preferred_element_type=F32) + b_ref[...])
    proj = jnp.dot(p_ref[...].astype(BF16), wp_ref[...], preferred_element_type=F32)
    out = h + gate * proj
    if final_norm:
        out = _rms_normalize(out, gf_ref[...], RMS_EPS)
    o_ref[...] = out


def _ple(h, gain, w_gate, b_gate, p, w_proj, gain_f, layer, final_norm):
    tm = _PLANS["ple"].tm
    m, d = h.shape
    e = p.shape[2]
    nblk = m // tm
    return pl.pallas_call(
        functools.partial(_ple_kernel, final_norm=final_norm),
        grid=(nblk,),
        in_specs=[
            pl.BlockSpec((tm, d), lambda i: (i, 0)),
            pl.BlockSpec((None, 1, d), lambda i: (layer, 0, 0)),
            pl.BlockSpec((None, d, d), lambda i: (layer, 0, 0),
                         pipeline_mode=pl.Buffered(1)),
            pl.BlockSpec((None, 1, d), lambda i: (layer, 0, 0)),
            pl.BlockSpec((None, tm, e), lambda i: (layer, i, 0)),
            pl.BlockSpec((None, e, d), lambda i: (layer, 0, 0),
                         pipeline_mode=pl.Buffered(1)),
            pl.BlockSpec((1, d), lambda i: (0, 0)),
        ],
        out_specs=pl.BlockSpec((tm, d), lambda i: (i, 0)),
        out_shape=jax.ShapeDtypeStruct((m, d), F32),
        compiler_params=_params(("parallel",), "ple"),
        name="ple",
    )(h, gain, w_gate, b_gate, p, w_proj, gain_f)


def _rows(start, dil):
    if dil == 1:
        return pl.ds(pl.multiple_of(start, A_BLOCK), A_BLOCK)
    return pl.ds(start, A_BLOCK, stride=dil)


def _dilated_attention_kernel(*refs):
    n_q = A_Q_PER_KV
    q_refs = refs[:n_q]
    n_kv = 4 * len(A_BRANCHES)
    kv_refs = refs[n_q:n_q + n_kv]
    o_ref = refs[n_q + n_kv]
    m_ref, l_ref, acc_ref, bias_ref = refs[n_q + n_kv + 1:]
    blk = A_BLOCK
    rows = n_q * blk
    exp2_scale = HEAD_DIM ** -0.5 * math.log2(math.e)

    qi = lax.broadcasted_iota(jnp.int32, (rows, 2 * blk), 0) % blk
    kj = lax.broadcasted_iota(jnp.int32, (rows, 2 * blk), 1)
    band = (kj >= qi) & (kj <= qi + blk)
    bias_ref[0] = jnp.where(band, 0.0, -jnp.inf)
    bias_ref[1] = jnp.where(band & (kj >= blk), 0.0, -jnp.inf)
    first_bias = (pl.program_id(1) == 0).astype(jnp.int32)

    def scores(dil, start, kp_ref, kp_start, kc_ref, vp_ref, vc_ref, bias_idx):
        sel = _rows(start, dil)
        psel = _rows(kp_start, dil)
        q = jnp.concatenate([q_ref[sel, :] for q_ref in q_refs], axis=0).astype(BF16)
        k = jnp.concatenate([kp_ref[psel, :], kc_ref[sel, :]], axis=0).astype(BF16)
        v = jnp.concatenate([vp_ref[psel, :], vc_ref[sel, :]], axis=0).astype(BF16)
        s = lax.dot_general(q, k, (((1,), (1,)), ((), ())),
                            preferred_element_type=F32) + bias_ref[bias_idx]
        return s, v

    def softmax_pv(s, v):
        m_blk = jnp.max(s, axis=-1, keepdims=True)
        p = jnp.exp2((s - m_blk) * exp2_scale)
        l_blk = jnp.sum(p, axis=-1, keepdims=True)
        return m_blk, l_blk, jnp.dot(p.astype(BF16), v, preferred_element_type=F32)

    def merge(dil, mode, start, m_blk, l_blk, pv):
        sel = _rows(start, dil)
        if mode == "init":
            m_new = jnp.broadcast_to(m_blk, (rows, HEAD_DIM))
            l_new = jnp.broadcast_to(l_blk, (rows, HEAD_DIM))
            acc_new = pv
        else:
            m_old = jnp.concatenate([m_ref.at[h][sel, :] for h in range(n_q)], axis=0)
            l_old = jnp.concatenate([l_ref.at[h][sel, :] for h in range(n_q)], axis=0)
            acc_old = jnp.concatenate([acc_ref.at[h][sel, :] for h in range(n_q)], axis=0)
            m_new = jnp.maximum(m_old, m_blk)
            alpha = jnp.exp2((m_old - m_new) * exp2_scale)
            beta = jnp.exp2((m_blk - m_new) * exp2_scale)
            l_new = alpha * l_old + beta * l_blk
            acc_new = alpha * acc_old + beta * pv
        for h in range(n_q):
            r = slice(h * blk, (h + 1) * blk)
            if mode == "final":
                o_ref[sel, h * HEAD_DIM:(h + 1) * HEAD_DIM] = (
                    acc_new[r] / l_new[r]).astype(o_ref.dtype)
            else:
                m_ref.at[h][sel, :] = m_new[r]
                l_ref.at[h][sel, :] = l_new[r]
                acc_ref.at[h][sel, :] = acc_new[r]

    def attend(dil, mode, blocks):
        loaded = [scores(dil, *block) for block in blocks]
        stats = [softmax_pv(s, v) for s, v in loaded]
        for block, stat in zip(blocks, stats):
            merge(dil, mode, block[0], *stat)

    modes = {16: "init", 4: "update", 1: "final"}
    for branch, (_, dil) in enumerate(A_BRANCHES):
        grp = A_BLOCKS_PER_TRIP[dil]
        kc_ref, kp_ref, vc_ref, vp_ref = kv_refs[4 * branch:4 * branch + 4]
        span = dil * blk
        n_later = dil * (A_TILE // span - 1)
        mode = modes[dil]

        def first_block(r, kc_ref=kc_ref, kp_ref=kp_ref, vc_ref=vc_ref, vp_ref=vp_ref):
            return (r, kp_ref, r, kc_ref, vp_ref, vc_ref, first_bias)

        def later_block(i, dil=dil, span=span, kc_ref=kc_ref, vc_ref=vc_ref):
            start = i % dil + span * (i // dil + 1)
            return (start, kc_ref, start - span, kc_ref, vc_ref, vc_ref, 0)

        def first_group(i, carry, dil=dil, mode=mode, grp=grp, first_block=first_block):
            attend(dil, mode, [first_block(grp * i + g) for g in range(grp)])
            return carry

        def later_group(i, carry, offset, dil=dil, mode=mode, grp=grp,
                        later_block=later_block):
            attend(dil, mode, [later_block(grp * i + g + offset) for g in range(grp)])
            return carry

        if dil > 1:
            lax.fori_loop(0, dil // grp, first_group, 0)
            if n_later:
                lax.fori_loop(0, n_later // grp, functools.partial(later_group, offset=0), 0)
        else:
            attend(dil, mode, [first_block(0)] + [later_block(g) for g in range(grp - 1)])
            lax.fori_loop(0, (n_later - grp + 1) // grp,
                          functools.partial(later_group, offset=grp - 1), 0)


def _dilated_attention(qkv, b, s):
    c = qkv.shape[-1]
    n_kv = (c // HEAD_DIM) // (A_Q_PER_KV + 2 * len(A_BRANCHES))
    n_qh = n_kv * A_Q_PER_KV
    d = n_qh * HEAD_DIM
    qkv = qkv.reshape(b, s, c)
    tiles = s // A_TILE

    def q_spec(h):
        return pl.BlockSpec((None, A_TILE, HEAD_DIM),
                            lambda bi, t, g: (bi, t, g * A_Q_PER_KV + h))

    def kv_specs(dil, is_v):
        col = n_qh + (2 * A_BRANCH_COLUMN[dil] + is_v) * n_kv
        ratio = A_TILE // (dil * A_BLOCK)
        cur = pl.BlockSpec((None, A_TILE, HEAD_DIM), lambda bi, t, g: (bi, t, col + g))
        prev = pl.BlockSpec((None, dil * A_BLOCK, HEAD_DIM),
                            lambda bi, t, g: (bi, jnp.maximum(t * ratio - 1, 0), col + g))
        return [cur, prev]

    in_specs = [q_spec(h) for h in range(A_Q_PER_KV)]
    for _, dil in A_BRANCHES:
        in_specs += kv_specs(dil, 0) + kv_specs(dil, 1)
    state = pltpu.VMEM((A_Q_PER_KV, A_TILE, HEAD_DIM), F32)
    out = pl.pallas_call(
        _dilated_attention_kernel,
        grid=(b, tiles, n_kv),
        in_specs=in_specs,
        out_specs=pl.BlockSpec((None, A_TILE, A_Q_PER_KV * HEAD_DIM),
                               lambda bi, t, g: (bi, t, g)),
        out_shape=jax.ShapeDtypeStruct((b, s, d), BF16),
        scratch_shapes=[state, state, state,
                        pltpu.VMEM((2, A_Q_PER_KV * A_BLOCK, 2 * A_BLOCK), F32)],
        compiler_params=_params(("parallel", "parallel", "parallel"), "dilated_attention"),
        name="dilated_attention",
    )(*([qkv] * len(in_specs)))
    return out.reshape(b * s, d)


def _diff_attention_kernel(q_ref, k_ref, v_ref, lam_ref, g_ref, o_ref,
                           vt_ref, acc_ref, sa_ref, sb_ref, *, tq, lambda_init):
    qi = pl.program_id(2)
    n_chunks = v_ref.shape[1] // tq
    exp2_scale = HEAD_DIM ** -0.5 * math.log2(math.e)

    @pl.when(qi == 0)
    def _():
        def transpose_chunk(kc, carry):
            start = pl.multiple_of(kc * tq, tq)
            vt_ref[:, pl.ds(start, tq)] = v_ref[0, pl.ds(start, tq), :].T
            return carry
        lax.fori_loop(0, n_chunks, transpose_chunk, 0)

    acc_ref[...] = jnp.zeros(acc_ref.shape, F32)
    q = q_ref[0]

    def scores_into(s_ref, kc):
        start = pl.multiple_of(kc * tq, tq)
        k = k_ref[0, pl.ds(start, tq), :]
        for c in range(2):
            lanes = slice(c * HEAD_DIM, (c + 1) * HEAD_DIM)
            s_ref[c] = lax.dot_general(k[:, lanes], q[:, lanes], (((1,), (1,)), ((), ())),
                                       preferred_element_type=F32)

    def attend(s_ref, kc, state, masked):
        start = pl.multiple_of(kc * tq, tq)
        vt = vt_ref[:, pl.ds(start, tq)]
        new_state = []
        for c in range(2):
            m_old, l_old = state[c]
            st = s_ref[c]
            if masked:
                key = lax.broadcasted_iota(jnp.int32, st.shape, 0)
                query = lax.broadcasted_iota(jnp.int32, st.shape, 1)
                st = jnp.where(key <= query, st, -jnp.inf)
            m_new = jnp.maximum(m_old, jnp.max(st, axis=0, keepdims=True))
            alpha = jnp.exp2((m_old - m_new) * exp2_scale)
            pt = jnp.exp2((st - m_new) * exp2_scale)
            l_new = alpha * l_old + jnp.sum(pt, axis=0, keepdims=True)
            acc_ref[c] = alpha * acc_ref[c] + jnp.dot(vt, pt.astype(BF16),
                                                      preferred_element_type=F32)
            new_state.append((m_new, l_new))
        return tuple(new_state)

    def chunk_pair(j, state):
        scores_into(sb_ref, 2 * j + 1)
        state = attend(sa_ref, 2 * j, state, False)
        scores_into(sa_ref, 2 * j + 2)
        return attend(sb_ref, 2 * j + 1, state, False)

    def even_tail(state):
        return attend(sa_ref, qi, state, True)

    def odd_tail(state):
        scores_into(sb_ref, qi)
        state = attend(sa_ref, qi - 1, state, False)
        return attend(sb_ref, qi, state, True)

    init = ((jnp.full((1, tq), -jnp.inf, F32), jnp.zeros((1, tq), F32)),) * 2
    scores_into(sa_ref, 0)
    def chunk_pairs(n):
        def body(j, st):
            for t in range(n):
                st = chunk_pair(n * j + t, st)
            return st
        return body

    state = lax.fori_loop(0, qi // 8, chunk_pairs(4), init)
    state = lax.fori_loop(2 * (qi // 8), qi // 4, chunk_pairs(2), state)
    state = lax.fori_loop(2 * (qi // 4), qi // 2, chunk_pair, state)
    (_, l0), (_, l1) = lax.cond(qi % 2 == 0, even_tail, odd_tail, state)

    lp = lam_ref[...]
    lam = (jnp.exp(jnp.sum(lp[0:1] * lp[1:2], axis=-1, keepdims=True))
           - jnp.exp(jnp.sum(lp[2:3] * lp[3:4], axis=-1, keepdims=True)) + lambda_init)
    ot = acc_ref[0] * (1.0 / l0) - lam * (acc_ref[1] * (1.0 / l1))
    o = _rms_normalize(ot.T, g_ref[...], SUBLN_EPS) * (1.0 - lambda_init)
    o_ref[0] = o.astype(o_ref.dtype)


def _diff_attention(qkv, lam_params, subln_g, wlayer, lambda_init, b, s):
    tq = _PLANS["diff_attention"].tm
    d = qkv.shape[-1] // 3
    hw = 2 * HEAD_DIM
    n_heads = d // hw
    qkv = qkv.reshape(b, s, 3 * d)
    out = pl.pallas_call(
        functools.partial(_diff_attention_kernel, tq=tq, lambda_init=lambda_init),
        grid=(b, n_heads, s // tq),
        in_specs=[
            pl.BlockSpec((1, tq, hw), lambda bi, h, i: (bi, i, h)),
            pl.BlockSpec((1, s, hw), lambda bi, h, i: (bi, 0, n_heads + h)),
            pl.BlockSpec((1, s, hw), lambda bi, h, i: (bi, 0, 2 * n_heads + h)),
            pl.BlockSpec((None, 4, HEAD_DIM), lambda bi, h, i: (wlayer, 0, 0)),
            pl.BlockSpec((None, 1, hw), lambda bi, h, i: (wlayer, 0, 0)),
        ],
        out_specs=pl.BlockSpec((1, tq, hw), lambda bi, h, i: (bi, i, h)),
        out_shape=jax.ShapeDtypeStruct((b, s, d), BF16),
        scratch_shapes=[pltpu.VMEM((hw, s), BF16), pltpu.VMEM((2, hw, tq), F32),
                        pltpu.VMEM((2, tq, tq), F32), pltpu.VMEM((2, tq, tq), F32)],
        compiler_params=_params(("parallel", "parallel", "arbitrary"), "diff_attention"),
        name="diff_attention",
    )(qkv, qkv, qkv, lam_params, subln_g)
    return out.reshape(b * s, d)


def _conv_proj_kernel(bg_ref, z_in_ref, zh_ref, cw_ref, w_ref, h_ref, o_ref,
                      z_ref, *, tiles_per_seq):
    tm = bg_ref.shape[0]
    halo = zh_ref.shape[0]
    seq_start = pl.program_id(0) % tiles_per_seq == 0
    z_ref[0:halo, :] = jnp.where(seq_start, 0.0, zh_ref[...])
    z_ref[halo:, :] = z_in_ref[...]
    cw = cw_ref[...]
    conv = (cw[0:1] * z_ref[pl.ds(halo - 2, tm), :]
            + cw[1:2] * z_ref[pl.ds(halo - 1, tm), :]
            + cw[2:3] * z_ref[pl.ds(halo, tm), :])
    y = (bg_ref[...] * conv).astype(BF16)
    o_ref[...] = h_ref[...] + jnp.dot(y, w_ref[...], preferred_element_type=F32)


def _conv_proj(proj, conv_w, w_out, wlayer, h, s):
    tm = _PLANS["conv_proj"].tm
    m, d = h.shape
    halo = F32_SUBLANES
    ratio = tm // halo
    return pl.pallas_call(
        functools.partial(_conv_proj_kernel, tiles_per_seq=s // tm),
        grid=(m // tm,),
        in_specs=[
            pl.BlockSpec((tm, d), lambda i: (i, 0)),
            pl.BlockSpec((tm, d), lambda i: (i, 1)),
            pl.BlockSpec((halo, d), lambda i: (jnp.maximum(i * ratio - 1, 0), 1)),
            pl.BlockSpec((None, 3, d), lambda i: (wlayer, 0, 0)),
            pl.BlockSpec((None, d, d), lambda i: (wlayer, 0, 0),
                         pipeline_mode=pl.Buffered(1)),
            pl.BlockSpec((tm, d), lambda i: (i, 0)),
        ],
        out_specs=pl.BlockSpec((tm, d), lambda i: (i, 0)),
        out_shape=jax.ShapeDtypeStruct((m, d), F32),
        scratch_shapes=[pltpu.VMEM((tm + halo, d), F32)],
        compiler_params=_params(("parallel",), "conv_proj"),
        name="conv_proj",
    )(proj, proj, proj, conv_w, w_out, h)


def kernel(x, p, norm_ffn1, w_ffn1_in, w_ffn1_out, norm_mix, a_w_qkv, a_w_o, b_w_qkv, b_w_o,
           b_lambda, b_subln, c_w_in, c_conv_w, c_w_out, norm_ffn2, w_ffn2_in, w_ffn2_out,
           norm_ple, w_ple_gate, b_ple_gate, w_ple_proj, norm_f):
    b, s, d = x.shape
    depth = p.shape[0]
    m = b * s
    h = x.reshape(m, d)
    p = p.reshape(depth, m, p.shape[-1])

    def row(t):
        return t.reshape(t.shape[0], 1, t.shape[-1])

    norm_ffn1, norm_mix, norm_ffn2, norm_ple, b_ple_gate, b_subln = (
        row(t) for t in (norm_ffn1, norm_mix, norm_ffn2, norm_ple, b_ple_gate, b_subln))
    norm_f = norm_f.reshape(1, d)
    a_w_qkv, a_w_o, b_w_qkv, b_w_o, c_w_in, c_w_out, w_ple_gate, w_ple_proj = _cast_many(
        [a_w_qkv, a_w_o, b_w_qkv, b_w_o, c_w_in, c_w_out, w_ple_gate, w_ple_proj])
    w_in = _cast_layers(w_ffn1_in, "cast_ffn_in", 0, 1)
    w_out = _cast_layers(w_ffn1_out, "cast_ffn_out", 0, 1)

    for i in range(depth):
        h, w_in, w_out = _ffn(h, norm_ffn1, i, w_in, w_out, 0, (w_ffn2_in, w_ffn2_out, i))
        kind, j = i % N_MIXERS, i // N_MIXERS
        if kind == 0:
            qkv = _norm_matmul(h, norm_mix, a_w_qkv, i, j, F32, "proj_a")
            mix = _dilated_attention(qkv, b, s)
            h = _proj_residual(mix, a_w_o, j, h)
        elif kind == 1:
            lambda_init = 0.8 - 0.6 * math.exp(-0.3 * i)
            qkv = _norm_matmul(h, norm_mix, b_w_qkv, i, j, BF16, "proj_b")
            mix = _diff_attention(qkv, b_lambda, b_subln, j, lambda_init, b, s)
            h = _proj_residual(mix, b_w_o, j, h)
        else:
            proj = _norm_matmul(h, norm_mix, c_w_in, i, j, F32, "proj_c", gated=True)
            h = _conv_proj(proj, c_conv_w, c_w_out, j, h, s)
        if i + 1 < depth:
            h, w_in, w_out = _ffn(h, norm_ffn2, i, w_in, w_out, 0,
                                  (w_ffn1_in, w_ffn1_out, i + 1))
        else:
            h, = _ffn(h, norm_ffn2, i, w_in, w_out, 0, None)
        h = _ple(h, norm_ple, w_ple_gate, b_ple_gate, p, w_ple_proj, norm_f, i,
                 i == depth - 1)
    return h.reshape(b, s, d)
```

```python
import functools
import math
from typing import NamedTuple

import jax
import jax.numpy as jnp
from jax import lax
from jax.experimental import pallas as pl
from jax.experimental.pallas import tpu as pltpu

BF16 = jnp.bfloat16
F32 = jnp.float32

HEAD_DIM = 128
A_Q_PER_KV = 4
A_BRANCHES = ((2048, 16), (512, 4), (128, 1))
A_BRANCH_COLUMN = {1: 0, 4: 1, 16: 2}
A_BLOCK = 128
A_TILE = 2048
A_BLOCKS_PER_TRIP = {16: 4, 4: 4, 1: 8}
RMS_EPS = 1e-6
SUBLN_EPS = 1e-5
N_MIXERS = 3

MIB = 1024 * 1024
V7X_VMEM_MIB = 64
F32_SUBLANES = 8


class _Plan(NamedTuple):
    tm: int
    tn: int
    vmem_mib: int


_PLANS = {
    "ffn": _Plan(1024, 512, 60),
    "proj_a": _Plan(512, 1024, 60),
    "proj_b": _Plan(512, 1024, 56),
    "proj_c": _Plan(512, 1024, 60),
    "out_proj": _Plan(512, 1024, 48),
    "conv_proj": _Plan(512, 0, 56),
    "ple": _Plan(512, 0, 56),
    "diff_attention": _Plan(512, 0, 40),
    "dilated_attention": _Plan(A_TILE, 0, 58),
    "cast_many": _Plan(32, 0, 40),
    "cast_ffn_in": _Plan(128, 0, 32),
    "cast_ffn_out": _Plan(512, 0, 32),
}
assert all(plan.vmem_mib < V7X_VMEM_MIB for plan in _PLANS.values())


def _params(semantics, plan):
    return pltpu.CompilerParams(dimension_semantics=semantics,
                                vmem_limit_bytes=_PLANS[plan].vmem_mib * MIB)


def _rms_normalize(x, gain, eps):
    ms = jnp.mean(x * x, axis=-1, keepdims=True)
    return x * lax.rsqrt(ms + eps) * gain


def _sigmoid(x):
    return 1.0 / (1.0 + jnp.exp(-x))


def _ffn_kernel(*refs, cast_next):
    if cast_next:
        (h_ref, g_ref, wg_ref, wu_ref, wo_ref, next_in_ref, next_out_ref,
         o_ref, next_in_bf_ref, next_out_bf_ref, hn_ref) = refs
        next_in_bf_ref[...] = next_in_ref[...].astype(BF16)
        next_out_bf_ref[...] = next_out_ref[...].astype(BF16)
    else:
        h_ref, g_ref, wg_ref, wu_ref, wo_ref, o_ref, hn_ref = refs

    def chunk(first):
        if first:
            h = h_ref[...]
            hn = _rms_normalize(h, g_ref[...], RMS_EPS).astype(BF16)
            hn_ref[...] = hn
        else:
            hn = hn_ref[...]
        gate = jnp.dot(hn, wg_ref[...], preferred_element_type=F32)
        up = jnp.dot(hn, wu_ref[...], preferred_element_type=F32)
        act = (0.5 * gate * _sigmoid(gate)) * up
        out = jnp.dot(act.astype(BF16), wo_ref[...], preferred_element_type=F32)
        o_ref[...] = (h if first else o_ref[...]) + out

    is_first = pl.program_id(1) == 0
    pl.when(is_first)(functools.partial(chunk, True))
    pl.when(jnp.logical_not(is_first))(functools.partial(chunk, False))


def _ffn(h, gain, layer, w_in, w_out, wlayer, next_w):
    tm, tf = _PLANS["ffn"][:2]
    m, d = h.shape
    f = w_out.shape[1]
    nf = f // tf
    n_i = m // tm
    in_specs = [
        pl.BlockSpec((tm, d), lambda i, j: (i, 0)),
        pl.BlockSpec((None, 1, d), lambda i, j: (layer, 0, 0)),
        pl.BlockSpec((None, d, tf), lambda i, j: (wlayer, 0, j)),
        pl.BlockSpec((None, d, tf), lambda i, j: (wlayer, 0, nf + j)),
        pl.BlockSpec((None, tf, d), lambda i, j: (wlayer, j, 0)),
    ]
    out_specs = [pl.BlockSpec((tm, d), lambda i, j: (i, 0))]
    out_shape = [jax.ShapeDtypeStruct((m, d), F32)]
    args = [h, gain, w_in, w_in, w_out]
    if next_w is not None:
        next_in, next_out, nl = next_w
        out_rows = f // (n_i * nf)
        in_specs += [pl.BlockSpec((None, d // n_i, 2 * tf), lambda i, j: (nl, i, j)),
                     pl.BlockSpec((None, out_rows, d), lambda i, j: (nl, i * nf + j, 0))]
        out_specs += [pl.BlockSpec((None, d // n_i, 2 * tf), lambda i, j: (0, i, j)),
                      pl.BlockSpec((None, out_rows, d), lambda i, j: (0, i * nf + j, 0))]
        out_shape += [jax.ShapeDtypeStruct((1, d, 2 * f), BF16),
                      jax.ShapeDtypeStruct((1, f, d), BF16)]
        args += [next_in, next_out]
    return pl.pallas_call(
        functools.partial(_ffn_kernel, cast_next=next_w is not None),
        grid=(n_i, nf),
        in_specs=in_specs,
        out_specs=out_specs,
        out_shape=out_shape,
        scratch_shapes=[pltpu.VMEM((tm, d), BF16)],
        compiler_params=_params(("parallel", "arbitrary"), "ffn"),
        name="ffn",
    )(*args)


def _norm_matmul_kernel(h_ref, g_ref, w_ref, o_ref, *, tn, gated):
    hn = _rms_normalize(h_ref[...], g_ref[...], RMS_EPS).astype(BF16)

    def cols(n0):
        return jnp.dot(hn, w_ref[:, n0:n0 + tn], preferred_element_type=F32)

    if gated:
        third = w_ref.shape[1] // 3
        for n0 in range(0, third, tn):
            o_ref[:, n0:n0 + tn] = cols(n0).astype(o_ref.dtype)
            o_ref[:, third + n0:third + n0 + tn] = (
                cols(third + n0) * cols(2 * third + n0)).astype(o_ref.dtype)
    else:
        for n0 in range(0, o_ref.shape[1], tn):
            o_ref[:, n0:n0 + tn] = cols(n0).astype(o_ref.dtype)


def _norm_matmul(h, gain, w, layer, wlayer, out_dtype, plan, gated=False):
    tm, tn = _PLANS[plan][:2]
    m, d = h.shape
    n = w.shape[2] // 3 * 2 if gated else w.shape[2]
    return pl.pallas_call(
        functools.partial(_norm_matmul_kernel, tn=tn, gated=gated),
        grid=(m // tm,),
        in_specs=[
            pl.BlockSpec((tm, d), lambda i: (i, 0)),
            pl.BlockSpec((None, 1, d), lambda i: (layer, 0, 0)),
            pl.BlockSpec((None, d, w.shape[2]), lambda i: (wlayer, 0, 0),
                         pipeline_mode=pl.Buffered(1)),
        ],
        out_specs=pl.BlockSpec((tm, n), lambda i: (i, 0)),
        out_shape=jax.ShapeDtypeStruct((m, n), out_dtype),
        compiler_params=_params(("parallel",), plan),
        name="norm_matmul",
    )(h, gain, w)


def _cast_kernel(x_ref, o_ref):
    o_ref[...] = x_ref[...].astype(o_ref.dtype)


def _cast_layers(w, plan, first=0, count=None):
    rows = min(_PLANS[plan].tm, w.shape[1])
    n_layers, r, c = w.shape
    count = n_layers if count is None else count
    return pl.pallas_call(
        _cast_kernel,
        grid=(count, r // rows),
        in_specs=[pl.BlockSpec((None, rows, c), lambda l, i: (first + l, i, 0))],
        out_specs=pl.BlockSpec((None, rows, c), lambda l, i: (l, i, 0)),
        out_shape=jax.ShapeDtypeStruct((count, r, c), BF16),
        compiler_params=_params(("parallel", "parallel"), plan),
        name="cast_layers",
    )(w)


def _cast_many_kernel(*refs):
    n = len(refs) // 2
    for x_ref, o_ref in zip(refs[:n], refs[n:]):
        o_ref[...] = x_ref[...].astype(o_ref.dtype)


def _cast_many(ws):
    steps = _PLANS["cast_many"].tm
    flat = [w.reshape(-1, w.shape[-1]) for w in ws]
    specs = [pl.BlockSpec((w.shape[0] // steps, w.shape[1]), lambda s: (s, 0)) for w in flat]
    outs = pl.pallas_call(
        _cast_many_kernel,
        grid=(steps,),
        in_specs=specs,
        out_specs=specs,
        out_shape=[jax.ShapeDtypeStruct(w.shape, BF16) for w in flat],
        compiler_params=_params(("parallel",), "cast_many"),
        name="cast_many",
    )(*flat)
    return [o.reshape(w.shape) for o, w in zip(outs, ws)]


def _proj_residual_kernel(x_ref, w_ref, h_ref, o_ref, *, tn):
    x = x_ref[...]
    for n0 in range(0, o_ref.shape[1], tn):
        cols = slice(n0, n0 + tn)
        o_ref[:, cols] = h_ref[:, cols] + jnp.dot(x, w_ref[:, cols],
                                                  preferred_element_type=F32)


def _proj_residual(x, w, wlayer, h):
    tm, tn = _PLANS["out_proj"][:2]
    m, k = x.shape
    n = w.shape[2]
    return pl.pallas_call(
        functools.partial(_proj_residual_kernel, tn=tn),
        grid=(m // tm,),
        in_specs=[
            pl.BlockSpec((tm, k), lambda i: (i, 0)),
            pl.BlockSpec((None, k, n), lambda i: (wlayer, 0, 0),
                         pipeline_mode=pl.Buffered(1)),
            pl.BlockSpec((tm, n), lambda i: (i, 0)),
        ],
        out_specs=pl.BlockSpec((tm, n), lambda i: (i, 0)),
        out_shape=jax.ShapeDtypeStruct((m, n), F32),
        compiler_params=_params(("parallel",), "out_proj"),
        name="proj_residual",
    )(x, w, h)


def _ple_kernel(h_ref, g_ref, wg_ref, b_ref, p_ref, wp_ref, gf_ref, o_ref, *, final_norm):
    h = h_ref[...]
    hn = _rms_normalize(h, g_ref[...], RMS_EPS).astype(BF16)
    gate = _sigmoid(jnp.dot(hn, wg_ref[...], preferred_element_type=F32) + b_ref[...])
    proj = jnp.dot(p_ref[...].astype(BF16), wp_ref[...], preferred_element_type=F32)
    out = h + gate * proj
    if final_norm:
        out = _rms_normalize(out, gf_ref[...], RMS_EPS)
    o_ref[...] = out


def _ple(h, gain, w_gate, b_gate, p, w_proj, gain_f, layer, final_norm):
    tm = _PLANS["ple"].tm
    m, d = h.shape
    e = p.shape[2]
    nblk = m // tm
    return pl.pallas_call(
        functools.partial(_ple_kernel, final_norm=final_norm),
        grid=(nblk,),
        in_specs=[
            pl.BlockSpec((tm, d), lambda i: (i, 0)),
            pl.BlockSpec((None, 1, d), lambda i: (layer, 0, 0)),
            pl.BlockSpec((None, d, d), lambda i: (layer, 0, 0),
                         pipeline_mode=pl.Buffered(1)),
            pl.BlockSpec((None, 1, d), lambda i: (layer, 0, 0)),
            pl.BlockSpec((None, tm, e), lambda i: (layer, i, 0)),
            pl.BlockSpec((None, e, d), lambda i: (layer, 0, 0),
                         pipeline_mode=pl.Buffered(1)),
            pl.BlockSpec((1, d), lambda i: (0, 0)),
        ],
        out_specs=pl.BlockSpec((tm, d), lambda i: (i, 0)),
        out_shape=jax.ShapeDtypeStruct((m, d), F32),
        compiler_params=_params(("parallel",), "ple"),
        name="ple",
    )(h, gain, w_gate, b_gate, p, w_proj, gain_f)


def _rows(start, dil):
    if dil == 1:
        return pl.ds(pl.multiple_of(start, A_BLOCK), A_BLOCK)
    return pl.ds(start, A_BLOCK, stride=dil)


def _dilated_attention_kernel(*refs):
    n_q = A_Q_PER_KV
    q_refs = refs[:n_q]
    n_kv = 4 * len(A_BRANCHES)
    kv_refs = refs[n_q:n_q + n_kv]
    o_ref = refs[n_q + n_kv]
    m_ref, l_ref, acc_ref, bias_ref = refs[n_q + n_kv + 1:]
    blk = A_BLOCK
    rows = n_q * blk
    exp2_scale = HEAD_DIM ** -0.5 * math.log2(math.e)

    qi = lax.broadcasted_iota(jnp.int32, (rows, 2 * blk), 0) % blk
    kj = lax.broadcasted_iota(jnp.int32, (rows, 2 * blk), 1)
    band = (kj >= qi) & (kj <= qi + blk)
    bias_ref[0] = jnp.where(band, 0.0, -jnp.inf)
    bias_ref[1] = jnp.where(band & (kj >= blk), 0.0, -jnp.inf)
    first_bias = (pl.program_id(1) == 0).astype(jnp.int32)

    def scores(dil, start, kp_ref, kp_start, kc_ref, vp_ref, vc_ref, bias_idx):
        sel = _rows(start, dil)
        psel = _rows(kp_start, dil)
        q = jnp.concatenate([q_ref[sel, :] for q_ref in q_refs], axis=0).astype(BF16)
        k = jnp.concatenate([kp_ref[psel, :], kc_ref[sel, :]], axis=0).astype(BF16)
        v = jnp.concatenate([vp_ref[psel, :], vc_ref[sel, :]], axis=0).astype(BF16)
        s = lax.dot_general(q, k, (((1,), (1,)), ((), ())),
                            preferred_element_type=F32) + bias_ref[bias_idx]
        return s, v

    def softmax_pv(s, v):
        m_blk = jnp.max(s, axis=-1, keepdims=True)
        p = jnp.exp2((s - m_blk) * exp2_scale).astype(BF16)
        v_ones = jnp.concatenate([v, jnp.ones_like(v)], axis=1)
        pv = jnp.dot(p, v_ones, preferred_element_type=F32)
        return m_blk, pv[:, HEAD_DIM:], pv[:, :HEAD_DIM]

    def merge(dil, mode, start, m_blk, l_blk, pv):
        sel = _rows(start, dil)
        if mode == "init":
            m_new = jnp.broadcast_to(m_blk, (rows, HEAD_DIM))
            l_new = jnp.broadcast_to(l_blk, (rows, HEAD_DIM))
            acc_new = pv
        else:
            m_old = jnp.concatenate([m_ref.at[h][sel, :] for h in range(n_q)], axis=0)
            l_old = jnp.concatenate([l_ref.at[h][sel, :] for h in range(n_q)], axis=0)
            acc_old = jnp.concatenate([acc_ref.at[h][sel, :] for h in range(n_q)], axis=0)
            m_new = jnp.maximum(m_old, m_blk)
            alpha = jnp.exp2((m_old - m_new) * exp2_scale)
            beta = jnp.exp2((m_blk - m_new) * exp2_scale)
            l_new = alpha * l_old + beta * l_blk
            acc_new = alpha * acc_old + beta * pv
        for h in range(n_q):
            r = slice(h * blk, (h + 1) * blk)
            if mode == "final":
                o_ref[sel, h * HEAD_DIM:(h + 1) * HEAD_DIM] = (
                    acc_new[r] / l_new[r]).astype(o_ref.dtype)
            else:
                m_ref.at[h][sel, :] = m_new[r]
                l_ref.at[h][sel, :] = l_new[r]
                acc_ref.at[h][sel, :] = acc_new[r]

    def attend(dil, mode, blocks):
        loaded = [scores(dil, *block) for block in blocks]
        stats = [softmax_pv(s, v) for s, v in loaded]
        for block, stat in zip(blocks, stats):
            merge(dil, mode, block[0], *stat)

    modes = {16: "init", 4: "update", 1: "final"}
    for branch, (_, dil) in enumerate(A_BRANCHES):
        grp = A_BLOCKS_PER_TRIP[dil]
        kc_ref, kp_ref, vc_ref, vp_ref = kv_refs[4 * branch:4 * branch + 4]
        span = dil * blk
        n_later = dil * (A_TILE // span - 1)
        mode = modes[dil]

        def first_block(r, kc_ref=kc_ref, kp_ref=kp_ref, vc_ref=vc_ref, vp_ref=vp_ref):
            return (r, kp_ref, r, kc_ref, vp_ref, vc_ref, first_bias)

        def later_block(i, dil=dil, span=span, kc_ref=kc_ref, vc_ref=vc_ref):
            start = i % dil + span * (i // dil + 1)
            return (start, kc_ref, start - span, kc_ref, vc_ref, vc_ref, 0)

        def first_group(i, carry, dil=dil, mode=mode, grp=grp, first_block=first_block):
            attend(dil, mode, [first_block(grp * i + g) for g in range(grp)])
            return carry

        def later_group(i, carry, offset, dil=dil, mode=mode, grp=grp,
                        later_block=later_block):
            attend(dil, mode, [later_block(grp * i + g + offset) for g in range(grp)])
            return carry

        if dil > 1:
            lax.fori_loop(0, dil // grp, first_group, 0)
            if n_later:
                lax.fori_loop(0, n_later // grp, functools.partial(later_group, offset=0), 0)
        else:
            attend(dil, mode, [first_block(0)] + [later_block(g) for g in range(grp - 1)])
            lax.fori_loop(0, (n_later - grp + 1) // grp,
                          functools.partial(later_group, offset=grp - 1), 0)


def _dilated_attention(qkv, b, s):
    c = qkv.shape[-1]
    n_kv = (c // HEAD_DIM) // (A_Q_PER_KV + 2 * len(A_BRANCHES))
    n_qh = n_kv * A_Q_PER_KV
    d = n_qh * HEAD_DIM
    qkv = qkv.reshape(b, s, c)
    tiles = s // A_TILE

    def q_spec(h):
        return pl.BlockSpec((None, A_TILE, HEAD_DIM),
                            lambda bi, t, g: (bi, t, g * A_Q_PER_KV + h))

    def kv_specs(dil, is_v):
        col = n_qh + (2 * A_BRANCH_COLUMN[dil] + is_v) * n_kv
        ratio = A_TILE // (dil * A_BLOCK)
        cur = pl.BlockSpec((None, A_TILE, HEAD_DIM), lambda bi, t, g: (bi, t, col + g))
        prev = pl.BlockSpec((None, dil * A_BLOCK, HEAD_DIM),
                            lambda bi, t, g: (bi, jnp.maximum(t * ratio - 1, 0), col + g))
        return [cur, prev]

    in_specs = [q_spec(h) for h in range(A_Q_PER_KV)]
    for _, dil in A_BRANCHES:
        in_specs += kv_specs(dil, 0) + kv_specs(dil, 1)
    state = pltpu.VMEM((A_Q_PER_KV, A_TILE, HEAD_DIM), F32)
    out = pl.pallas_call(
        _dilated_attention_kernel,
        grid=(b, tiles, n_kv),
        in_specs=in_specs,
        out_specs=pl.BlockSpec((None, A_TILE, A_Q_PER_KV * HEAD_DIM),
                               lambda bi, t, g: (bi, t, g)),
        out_shape=jax.ShapeDtypeStruct((b, s, d), BF16),
        scratch_shapes=[state, state, state,
                        pltpu.VMEM((2, A_Q_PER_KV * A_BLOCK, 2 * A_BLOCK), F32)],
        compiler_params=_params(("parallel", "parallel", "parallel"), "dilated_attention"),
        name="dilated_attention",
    )(*([qkv] * len(in_specs)))
    return out.reshape(b * s, d)


def _diff_attention_kernel(q_ref, k_ref, v_ref, lam_ref, g_ref, o_ref,
                           vt_ref, acc_ref, sa_ref, sb_ref, *, tq, lambda_init):
    qi = pl.program_id(2)
    n_chunks = v_ref.shape[1] // tq
    exp2_scale = HEAD_DIM ** -0.5 * math.log2(math.e)

    @pl.when(qi == 0)
    def _():
        def transpose_chunk(kc, carry):
            start = pl.multiple_of(kc * tq, tq)
            vt_ref[:, pl.ds(start, tq)] = v_ref[0, pl.ds(start, tq), :].T
            return carry
        lax.fori_loop(0, n_chunks, transpose_chunk, 0)

    acc_ref[...] = jnp.zeros(acc_ref.shape, F32)
    q = q_ref[0]

    def scores_into(s_ref, kc):
        start = pl.multiple_of(kc * tq, tq)
        k = k_ref[0, pl.ds(start, tq), :]
        for c in range(2):
            lanes = slice(c * HEAD_DIM, (c + 1) * HEAD_DIM)
            s_ref[c] = lax.dot_general(k[:, lanes], q[:, lanes], (((1,), (1,)), ((), ())),
                                       preferred_element_type=F32)

    def attend(s_ref, kc, state, masked):
        start = pl.multiple_of(kc * tq, tq)
        vt = vt_ref[:, pl.ds(start, tq)]
        new_state = []
        for c in range(2):
            m_old, l_old = state[c]
            st = s_ref[c]
            if masked:
                key = lax.broadcasted_iota(jnp.int32, st.shape, 0)
                query = lax.broadcasted_iota(jnp.int32, st.shape, 1)
                st = jnp.where(key <= query, st, -jnp.inf)
            m_new = jnp.maximum(m_old, jnp.max(st, axis=0, keepdims=True))
            alpha = jnp.exp2((m_old - m_new) * exp2_scale)
            pt = jnp.exp2((st - m_new) * exp2_scale)
            l_new = alpha * l_old + jnp.sum(pt, axis=0, keepdims=True)
            acc_ref[c] = alpha * acc_ref[c] + jnp.dot(vt, pt.astype(BF16),
                                                      preferred_element_type=F32)
            new_state.append((m_new, l_new))
        return tuple(new_state)

    def chunk_pair(j, state):
        scores_into(sb_ref, 2 * j + 1)
        state = attend(sa_ref, 2 * j, state, False)
        scores_into(sa_ref, 2 * j + 2)
        return attend(sb_ref, 2 * j + 1, state, False)

    def even_tail(state):
        return attend(sa_ref, qi, state, True)

    def odd_tail(state):
        scores_into(sb_ref, qi)
        state = attend(sa_ref, qi - 1, state, False)
        return attend(sb_ref, qi, state, True)

    init = ((jnp.full((1, tq), -jnp.inf, F32), jnp.zeros((1, tq), F32)),) * 2
    scores_into(sa_ref, 0)
    def chunk_pairs(n):
        def body(j, st):
            for t in range(n):
                st = chunk_pair(n * j + t, st)
            return st
        return body

    state = lax.fori_loop(0, qi // 8, chunk_pairs(4), init)
    state = lax.fori_loop(2 * (qi // 8), qi // 4, chunk_pairs(2), state)
    state = lax.fori_loop(2 * (qi // 4), qi // 2, chunk_pair, state)
    (_, l0), (_, l1) = lax.cond(qi % 2 == 0, even_tail, odd_tail, state)

    lp = lam_ref[...]
    lam = (jnp.exp(jnp.sum(lp[0:1] * lp[1:2], axis=-1, keepdims=True))
           - jnp.exp(jnp.sum(lp[2:3] * lp[3:4], axis=-1, keepdims=True)) + lambda_init)
    ot = acc_ref[0] * (1.0 / l0) - lam * (acc_ref[1] * (1.0 / l1))
    o = _rms_normalize(ot.T, g_ref[...], SUBLN_EPS) * (1.0 - lambda_init)
    o_ref[0] = o.astype(o_ref.dtype)


def _diff_attention(qkv, lam_params, subln_g, wlayer, lambda_init, b, s):
    tq = _PLANS["diff_attention"].tm
    d = qkv.shape[-1] // 3
    hw = 2 * HEAD_DIM
    n_heads = d // hw
    qkv = qkv.reshape(b, s, 3 * d)
    out = pl.pallas_call(
        functools.partial(_diff_attention_kernel, tq=tq, lambda_init=lambda_init),
        grid=(b, n_heads, s // tq),
        in_specs=[
            pl.BlockSpec((1, tq, hw), lambda bi, h, i: (bi, i, h)),
            pl.BlockSpec((1, s, hw), lambda bi, h, i: (bi, 0, n_heads + h)),
            pl.BlockSpec((1, s, hw), lambda bi, h, i: (bi, 0, 2 * n_heads + h)),
            pl.BlockSpec((None, 4, HEAD_DIM), lambda bi, h, i: (wlayer, 0, 0)),
            pl.BlockSpec((None, 1, hw), lambda bi, h, i: (wlayer, 0, 0)),
        ],
        out_specs=pl.BlockSpec((1, tq, hw), lambda bi, h, i: (bi, i, h)),
        out_shape=jax.ShapeDtypeStruct((b, s, d), BF16),
        scratch_shapes=[pltpu.VMEM((hw, s), BF16), pltpu.VMEM((2, hw, tq), F32),
                        pltpu.VMEM((2, tq, tq), F32), pltpu.VMEM((2, tq, tq), F32)],
        compiler_params=_params(("parallel", "parallel", "arbitrary"), "diff_attention"),
        name="diff_attention",
    )(qkv, qkv, qkv, lam_params, subln_g)
    return out.reshape(b * s, d)


def _conv_proj_kernel(bg_ref, z_in_ref, zh_ref, cw_ref, w_ref, h_ref, o_ref,
                      z_ref, *, tiles_per_seq):
    tm = bg_ref.shape[0]
    halo = zh_ref.shape[0]
    seq_start = pl.program_id(0) % tiles_per_seq == 0
    z_ref[0:halo, :] = jnp.where(seq_start, 0.0, zh_ref[...])
    z_ref[halo:, :] = z_in_ref[...]
    cw = cw_ref[...]
    conv = (cw[0:1] * z_ref[pl.ds(halo - 2, tm), :]
            + cw[1:2] * z_ref[pl.ds(halo - 1, tm), :]
            + cw[2:3] * z_ref[pl.ds(halo, tm), :])
    y = (bg_ref[...] * conv).astype(BF16)
    o_ref[...] = h_ref[...] + jnp.dot(y, w_ref[...], preferred_element_type=F32)


def _conv_proj(proj, conv_w, w_out, wlayer, h, s):
    tm = _PLANS["conv_proj"].tm
    m, d = h.shape
    halo = F32_SUBLANES
    ratio = tm // halo
    return pl.pallas_call(
        functools.partial(_conv_proj_kernel, tiles_per_seq=s // tm),
        grid=(m // tm,),
        in_specs=[
            pl.BlockSpec((tm, d), lambda i: (i, 0)),
            pl.BlockSpec((tm, d), lambda i: (i, 1)),
            pl.BlockSpec((halo, d), lambda i: (jnp.maximum(i * ratio - 1, 0), 1)),
            pl.BlockSpec((None, 3, d), lambda i: (wlayer, 0, 0)),
            pl.BlockSpec((None, d, d), lambda i: (wlayer, 0, 0),
                         pipeline_mode=pl.Buffered(1)),
            pl.BlockSpec((tm, d), lambda i: (i, 0)),
        ],
        out_specs=pl.BlockSpec((tm, d), lambda i: (i, 0)),
        out_shape=jax.ShapeDtypeStruct((m, d), F32),
        scratch_shapes=[pltpu.VMEM((tm + halo, d), F32)],
        compiler_params=_params(("parallel",), "conv_proj"),
        name="conv_proj",
    )(proj, proj, proj, conv_w, w_out, h)


def kernel(x, p, norm_ffn1, w_ffn1_in, w_ffn1_out, norm_mix, a_w_qkv, a_w_o, b_w_qkv, b_w_o,
           b_lambda, b_subln, c_w_in, c_conv_w, c_w_out, norm_ffn2, w_ffn2_in, w_ffn2_out,
           norm_ple, w_ple_gate, b_ple_gate, w_ple_proj, norm_f):
    b, s, d = x.shape
    depth = p.shape[0]
    m = b * s
    h = x.reshape(m, d)
    p = p.reshape(depth, m, p.shape[-1])

    def row(t):
        return t.reshape(t.shape[0], 1, t.shape[-1])

    norm_ffn1, norm_mix, norm_ffn2, norm_ple, b_ple_gate, b_subln = (
        row(t) for t in (norm_ffn1, norm_mix, norm_ffn2, norm_ple, b_ple_gate, b_subln))
    norm_f = norm_f.reshape(1, d)
    a_w_qkv, a_w_o, b_w_qkv, b_w_o, c_w_in, c_w_out, w_ple_gate, w_ple_proj = _cast_many(
        [a_w_qkv, a_w_o, b_w_qkv, b_w_o, c_w_in, c_w_out, w_ple_gate, w_ple_proj])
    w_in = _cast_layers(w_ffn1_in, "cast_ffn_in", 0, 1)
    w_out = _cast_layers(w_ffn1_out, "cast_ffn_out", 0, 1)

    for i in range(depth):
        h, w_in, w_out = _ffn(h, norm_ffn1, i, w_in, w_out, 0, (w_ffn2_in, w_ffn2_out, i))
        kind, j = i % N_MIXERS, i // N_MIXERS
        if kind == 0:
            qkv = _norm_matmul(h, norm_mix, a_w_qkv, i, j, F32, "proj_a")
            mix = _dilated_attention(qkv, b, s)
            h = _proj_residual(mix, a_w_o, j, h)
        elif kind == 1:
            lambda_init = 0.8 - 0.6 * math.exp(-0.3 * i)
            qkv = _norm_matmul(h, norm_mix, b_w_qkv, i, j, BF16, "proj_b")
            mix = _diff_attention(qkv, b_lambda, b_subln, j, lambda_init, b, s)
            h = _proj_residual(mix, b_w_o, j, h)
        else:
            proj = _norm_matmul(h, norm_mix, c_w_in, i, j, F32, "proj_c", gated=True)
            h = _conv_proj(proj, c_conv_w, c_w_out, j, h, s)
        if i + 1 < depth:
            h, w_in, w_out = _ffn(h, norm_ffn2, i, w_in, w_out, 0,
                                  (w_ffn1_in, w_ffn1_out, i + 1))
        else:
            h, = _ffn(h, norm_ffn2, i, w_in, w_out, 0, None)
        h = _ple(h, norm_ple, w_ple_gate, b_ple_gate, p, w_ple_proj, norm_f, i,
                 i == depth - 1)
    return h.reshape(b, s, d)
```
